```python
import math
import jax, jax.numpy as jnp
from jax import lax
import numpy as np

D_MODEL = 1024
BATCH = 16
SEQ = 2048
DEPTH = 2
DEC_BATCH = 2
DEC_SEQ = 16384
PAST_LEN = 128

MLA_HEADS = 8
MLA_NOPE = 64
MLA_ROPE = 32
MLA_V = 64
MLA_Q_LORA = 384
MLA_KV_LORA = 256
ROPE_THETA = 10000.0
Q_BLOCK = 128
SWA_HEADS = 8
SWA_KV_HEADS = 2
SWA_HEAD_DIM = 64
WINDOW = 128
REL_BUCKETS = 32
REL_MAX_DIST = 128
GLA_HEADS = 4
GLA_DK = D_MODEL // 2 // GLA_HEADS
GLA_DV = D_MODEL // GLA_HEADS
GLA_GATE_RANK = 16
GLA_GATE_NORM = 16.0
GLA_CHUNK = 64
D_FF = 2816
N_EXPERTS = 8
TOP_K = 2
D_FF_EXPERT = 3584
MOE_BLOCK = 128
EPS = 1e-6

AB_COLS = MLA_Q_LORA + MLA_KV_LORA + MLA_ROPE + SWA_HEADS * SWA_HEAD_DIM + 2 * SWA_KV_HEADS * SWA_HEAD_DIM
MIX_AB = MLA_HEADS * MLA_V + SWA_HEADS * SWA_HEAD_DIM
C_COLS = 2 * GLA_HEADS * GLA_DK + 2 * GLA_HEADS * GLA_DV + 2 * GLA_GATE_RANK
MIX_C = GLA_HEADS * GLA_DV

kernel_name = 'hybrid_mla_swa_gla_moe_encoder'


def rms_norm(x, g):
    xf = x.astype(jnp.float32)
    y = xf * lax.rsqrt(jnp.mean(xf * xf, axis=-1, keepdims=True) + EPS)
    return (y * g.astype(jnp.float32)).astype(x.dtype)


def modulate(x, g, shift, scale):
    return rms_norm(x, g) * (1.0 + scale[:, None, :]) + shift[:, None, :]


def rope(x, pos):
    half = x.shape[-1] // 2
    inv = 1.0 / (ROPE_THETA ** (jnp.arange(half, dtype=jnp.float32) / half))
    ang = pos[:, None] * inv[None, :]
    cos = jnp.cos(ang)[None, :, None, :]
    sin = jnp.sin(ang)[None, :, None, :]
    xf = x.astype(jnp.float32)
    x1, x2 = xf[..., :half], xf[..., half:]
    return jnp.concatenate([x1 * cos - x2 * sin, x1 * sin + x2 * cos], axis=-1).astype(x.dtype)


def t5_bucket(rel):
    nb = REL_BUCKETS // 2
    max_exact = nb // 2
    n = jnp.abs(rel)
    big = max_exact + (jnp.log(jnp.maximum(n, 1).astype(jnp.float32) / max_exact)
                       / math.log(REL_MAX_DIST / max_exact) * (nb - max_exact)).astype(jnp.int32)
    big = jnp.minimum(big, nb - 1)
    return jnp.where(rel > 0, nb, 0) + jnp.where(n < max_exact, n, big)


def mla(xq_a, xkv_a, q_norm, kv_norm, w_qb, w_kvb):
    B, T, _ = xq_a.shape
    dqk = MLA_NOPE + MLA_ROPE
    pos = jnp.arange(T, dtype=jnp.float32)
    q = (rms_norm(xq_a, q_norm) @ w_qb).reshape(B, T, MLA_HEADS, dqk)
    q = jnp.concatenate([q[..., :MLA_NOPE], rope(q[..., MLA_NOPE:], pos)], axis=-1)
    ckv = rms_norm(xkv_a[..., :MLA_KV_LORA], kv_norm)
    k_pe = rope(xkv_a[..., MLA_KV_LORA:][:, :, None, :], pos)
    kv = (ckv @ w_kvb).reshape(B, T, MLA_HEADS, MLA_NOPE + MLA_V)
    k = jnp.concatenate([kv[..., :MLA_NOPE], jnp.broadcast_to(k_pe, (B, T, MLA_HEADS, MLA_ROPE))], axis=-1)
    v = kv[..., MLA_NOPE:]
    qb = (q * (dqk ** -0.5)).reshape(B, T // Q_BLOCK, Q_BLOCK, MLA_HEADS, dqk).transpose(1, 0, 2, 3, 4)

    def attend(qblk):
        s = jnp.einsum('bqhd,bkhd->bhqk', qblk, k).astype(jnp.float32)
        p = jax.nn.softmax(s, axis=-1).astype(v.dtype)
        return jnp.einsum('bhqk,bkhd->bqhd', p, v)

    o = lax.map(attend, qb)
    return o.transpose(1, 0, 2, 3, 4).reshape(B, T, MLA_HEADS * MLA_V)


def window_gqa(q, k, v, sink, rel_bias):
    B, T, _ = q.shape
    W = WINDOW
    nb = T // W
    G = SWA_HEADS // SWA_KV_HEADS
    hd = SWA_HEAD_DIM
    qb = q.reshape(B, nb, W, SWA_KV_HEADS, G, hd) * (hd ** -0.5)

    def band(t):
        t = t.reshape(B, T, SWA_KV_HEADS, hd)
        tp = jnp.pad(t, ((0, 0), (W, W), (0, 0), (0, 0))).reshape(B, nb + 2, W, SWA_KV_HEADS, hd)
        return jnp.concatenate([tp[:, :-2], tp[:, 1:-1], tp[:, 2:]], axis=2)

    kb, vb = band(k), band(v)
    s = jnp.einsum('bnqhgd,bnjhd->bnhgqj', qb, kb).astype(jnp.float32)
    qi = jnp.arange(W, dtype=jnp.int32)[:, None]
    kj = jnp.arange(3 * W, dtype=jnp.int32)[None, :]
    rel = kj - W - qi
    bias = rel_bias[t5_bucket(rel)].astype(jnp.float32)
    bias = bias.transpose(2, 0, 1).reshape(SWA_KV_HEADS, G, W, 3 * W)
    kpos = jnp.arange(nb, dtype=jnp.int32)[:, None] * W - W + kj
    valid = (jnp.abs(rel) <= W)[None] & ((kpos >= 0) & (kpos < T))[:, None, :]
    s = jnp.where(valid[None, :, None, None], s + bias, -jnp.inf)
    sink_l = sink.astype(jnp.float32).reshape(SWA_KV_HEADS, G)[None, None, :, :, None, None]
    m = jnp.maximum(jnp.max(s, axis=-1, keepdims=True), sink_l)
    p = jnp.exp(s - m)
    p = p / (jnp.sum(p, axis=-1, keepdims=True) + jnp.exp(sink_l - m))
    o = jnp.einsum('bnhgqj,bnjhd->bnqhgd', p.astype(vb.dtype), vb)
    return o.reshape(B, T, SWA_HEADS * hd)


def mixer_ab(h, w_in, q_norm, kv_norm, w_qb, w_kvb, sink, rel_bias, w_out):
    proj = h @ w_in
    o1 = MLA_Q_LORA
    o2 = o1 + MLA_KV_LORA + MLA_ROPE
    o3 = o2 + SWA_HEADS * SWA_HEAD_DIM
    o4 = o3 + SWA_KV_HEADS * SWA_HEAD_DIM
    a = mla(proj[..., :o1], proj[..., o1:o2], q_norm, kv_norm, w_qb, w_kvb)
    b = window_gqa(proj[..., o2:o3], proj[..., o3:o4], proj[..., o4:], sink, rel_bias)
    return jnp.concatenate([a, b], axis=-1) @ w_out


def gla_scan(q, k, v, g):
    B, H, T, dk = q.shape
    dv = v.shape[-1]
    L = GLA_CHUNK
    n = T // L
    q, k, g = (t.reshape(B, H, n, L, dk) for t in (q, k, g))
    v = v.reshape(B, H, n, L, dv)
    b = jnp.cumsum(g, axis=3)
    b_last = b[:, :, :, -1:, :]
    qe = q * jnp.exp(b)
    ke = k * jnp.exp(-b)
    kd = k * jnp.exp(b_last - b)
    a = jnp.einsum('bhnid,bhnjd->bhnij', qe, ke)
    a = jnp.where(jnp.tril(jnp.ones((L, L), dtype=bool)), a, 0.0)
    o_intra = jnp.einsum('bhnij,bhnjv->bhniv', a, v)

    def step(state, xs):
        qe_n, kd_n, v_n, dec_n = xs
        o_n = jnp.einsum('bhid,bhdv->bhiv', qe_n, state)
        state = dec_n[..., None] * state + jnp.einsum('bhjd,bhjv->bhdv', kd_n, v_n)
        return state, o_n

    s0 = jnp.zeros((B, H, dk, dv), jnp.float32)
    xs = (jnp.moveaxis(qe, 2, 0), jnp.moveaxis(kd, 2, 0), jnp.moveaxis(v, 2, 0),
          jnp.moveaxis(jnp.exp(b_last[:, :, :, 0, :]), 2, 0))
    _, o_inter = lax.scan(step, s0, xs)
    return (o_intra + jnp.moveaxis(o_inter, 0, 2)).reshape(B, H, T, dv)


def mixer_c(h, w_in, w_gate_up, b_gate_up, out_norm, w_out):
    B, T, _ = h.shape
    proj = h @ w_in
    nk = GLA_HEADS * GLA_DK
    nv = GLA_HEADS * GLA_DV
    q = proj[..., :nk]
    k = proj[..., nk:2 * nk]
    v = proj[..., 2 * nk:2 * nk + nv]
    r = proj[..., 2 * nk + nv:2 * nk + 2 * nv]
    gd_f = proj[..., 2 * nk + 2 * nv:2 * nk + 2 * nv + GLA_GATE_RANK]
    gd_b = proj[..., 2 * nk + 2 * nv + GLA_GATE_RANK:]

    def heads(t, d):
        return t.reshape(B, T, GLA_HEADS, d).transpose(0, 2, 1, 3).astype(jnp.float32)

    def log_decay(gd, wu, bu):
        pre = (gd @ wu + bu).astype(jnp.float32)
        return heads(jax.nn.log_sigmoid(pre) / GLA_GATE_NORM, GLA_DK)

    qh = heads(q, GLA_DK) * (GLA_DK ** -0.5)
    kh = heads(k, GLA_DK)
    vh = heads(v, GLA_DV)
    g_f = log_decay(gd_f, w_gate_up[0], b_gate_up[0])
    g_b = log_decay(gd_b, w_gate_up[1], b_gate_up[1])
    flip = lambda t: jnp.flip(t, axis=2)
    o = gla_scan(qh, kh, vh, g_f) + flip(gla_scan(flip(qh), flip(kh), flip(vh), flip(g_b)))
    o = rms_norm(o, out_norm)
    o = o.transpose(0, 2, 1, 3).reshape(B, T, nv).astype(h.dtype)
    return (o * jax.nn.silu(r)) @ w_out


def swiglu(h, w_gate, w_up, w_down):
    return (jax.nn.silu(h @ w_gate) * (h @ w_up)) @ w_down


def moe_swiglu(h, w_router, w_gate, w_up, w_down):
    n, d = h.shape
    logits = (h @ w_router).astype(jnp.float32)
    top_logit, top_idx = lax.top_k(logits, TOP_K)
    gates = jax.nn.softmax(top_logit, axis=-1)
    flat_e = top_idx.reshape(-1).astype(jnp.int32)
    flat_tok = jnp.arange(n * TOP_K, dtype=jnp.int32) // TOP_K
    flat_g = gates.reshape(-1)
    order = jnp.argsort(flat_e)
    sorted_e = flat_e[order]
    counts = jnp.bincount(flat_e, length=N_EXPERTS).astype(jnp.int32)
    padded = (counts + MOE_BLOCK - 1) // MOE_BLOCK * MOE_BLOCK
    start = jnp.cumsum(counts) - counts
    pad_end = jnp.cumsum(padded)
    pad_start = pad_end - padded
    rank = jnp.arange(n * TOP_K, dtype=jnp.int32) - start[sorted_e]
    dest = pad_start[sorted_e] + rank
    rows = n * TOP_K + N_EXPERTS * MOE_BLOCK
    n_blocks = rows // MOE_BLOCK
    tok_buf = jnp.zeros((rows,), jnp.int32).at[dest].set(flat_tok[order])
    gate_buf = jnp.zeros((rows,), jnp.float32).at[dest].set(flat_g[order])
    blk_start = jnp.arange(n_blocks, dtype=jnp.int32) * MOE_BLOCK
    blk_e = jnp.minimum(jnp.searchsorted(pad_end, blk_start, side='right'), N_EXPERTS - 1)
    xb = h[tok_buf].reshape(n_blocks, MOE_BLOCK, d)

    def expert_block(args):
        xblk, e = args
        return (jax.nn.silu(xblk @ w_gate[e]) * (xblk @ w_up[e])) @ w_down[e]

    yb = lax.map(expert_block, (xb, blk_e)).reshape(rows, d)
    out = jnp.zeros((n, d), jnp.float32).at[tok_buf].add(yb.astype(jnp.float32) * gate_buf[:, None])
    return out.astype(h.dtype)


def trunk(x, c, norm_gain, w_ada, b_ada, rel_bias, w_in_ab, mla_q_norm, mla_kv_norm, mla_w_qb, mla_w_kvb,
          swa_sink, w_out_ab, ffn_w_gate, ffn_w_up, ffn_w_down, w_in_c, gla_w_gate_up, gla_b_gate_up,
          gla_out_norm, w_out_c, moe_router, moe_w_gate, moe_w_up, moe_w_down):
    B, T, D = x.shape
    for l in range(DEPTH):
        i = l // 2
        mod = jax.nn.silu(c) @ w_ada[l] + b_ada[l]
        sh1, sc1, g1, sh2, sc2, g2 = jnp.split(mod, 6, axis=-1)
        h = modulate(x, norm_gain[l, 0], sh1, sc1)
        if l % 2 == 0:
            y = mixer_ab(h, w_in_ab[i], mla_q_norm[i], mla_kv_norm[i], mla_w_qb[i], mla_w_kvb[i],
                         swa_sink[i], rel_bias, w_out_ab[i])
        else:
            y = mixer_c(h, w_in_c[i], gla_w_gate_up[i], gla_b_gate_up[i], gla_out_norm[i], w_out_c[i])
        x = x + g1[:, None, :] * rms_norm(y, norm_gain[l, 1])
        h = modulate(x, norm_gain[l, 2], sh2, sc2)
        if l % 2 == 0:
            y = swiglu(h, ffn_w_gate[i], ffn_w_up[i], ffn_w_down[i])
        else:
            y = moe_swiglu(h.reshape(B * T, D), moe_router[i], moe_w_gate[i], moe_w_up[i],
                           moe_w_down[i]).reshape(B, T, D)
        x = x + g2[:, None, :] * rms_norm(y, norm_gain[l, 3])
    return x


def setup_inputs(seed: int = 0) -> dict:
    key = jax.random.key(seed)
    ks = jax.random.split(key, 32)
    ne = (DEPTH + 1) // 2
    no = DEPTH // 2
    D = D_MODEL
    f32 = jnp.float32

    def nrm(k, shape, s=1.0):
        return jax.random.normal(k, shape, f32) * s

    def w(k, shape, fan_in):
        return jax.random.normal(k, shape, f32) * fan_in ** -0.5

    def gain(k, shape):
        return 1.0 + 0.05 * jax.random.normal(k, shape, f32)

    return {
        'x_prompt': nrm(ks[0], (BATCH, SEQ, D)),
        'x_sample': nrm(ks[1], (DEC_BATCH, DEC_SEQ, D)),
        'c_prompt': nrm(ks[2], (BATCH, D)),
        'c_sample': nrm(ks[3], (DEC_BATCH, D)),
        'norm_gain': gain(ks[4], (DEPTH, 4, D)),
        'w_ada': w(ks[5], (DEPTH, D, 6 * D), D) * 0.5,
        'b_ada': nrm(ks[6], (DEPTH, 6 * D), 0.02),
        'rel_bias': nrm(ks[7], (REL_BUCKETS, SWA_HEADS), 0.5),
        'w_in_ab': w(ks[8], (ne, D, AB_COLS), D),
        'mla_q_norm': gain(ks[9], (ne, MLA_Q_LORA)),
        'mla_kv_norm': gain(ks[10], (ne, MLA_KV_LORA)),
        'mla_w_qb': w(ks[11], (ne, MLA_Q_LORA, MLA_HEADS * (MLA_NOPE + MLA_ROPE)), MLA_Q_LORA),
        'mla_w_kvb': w(ks[12], (ne, MLA_KV_LORA, MLA_HEADS * (MLA_NOPE + MLA_V)), MLA_KV_LORA),
        'swa_sink': nrm(ks[13], (ne, SWA_HEADS), 0.5),
        'w_out_ab': w(ks[14], (ne, MIX_AB, D), MIX_AB),
        'ffn_w_gate': w(ks[15], (ne, D, D_FF), D),
        'ffn_w_up': w(ks[16], (ne, D, D_FF), D),
        'ffn_w_down': w(ks[17], (ne, D_FF, D), D_FF),
        'w_in_c': w(ks[18], (no, D, C_COLS), D),
        'gla_w_gate_up': w(ks[19], (no, 2, GLA_GATE_RANK, GLA_HEADS * GLA_DK), GLA_GATE_RANK),
        'gla_b_gate_up': nrm(ks[20], (no, 2, GLA_HEADS * GLA_DK), 0.1),
        'gla_out_norm': gain(ks[21], (no, GLA_DV)),
        'w_out_c': w(ks[22], (no, MIX_C, D), MIX_C),
        'moe_router': w(ks[23], (no, D, N_EXPERTS), D),
        'moe_w_gate': w(ks[24], (no, N_EXPERTS, D, D_FF_EXPERT), D),
        'moe_w_up': w(ks[25], (no, N_EXPERTS, D, D_FF_EXPERT), D),
        'moe_w_down': w(ks[26], (no, N_EXPERTS, D_FF_EXPERT, D), D_FF_EXPERT),
    }


def reference(x_prompt, x_sample, c_prompt, c_sample, norm_gain, w_ada, b_ada, rel_bias, w_in_ab,
              mla_q_norm, mla_kv_norm, mla_w_qb, mla_w_kvb, swa_sink, w_out_ab, ffn_w_gate, ffn_w_up,
              ffn_w_down, w_in_c, gla_w_gate_up, gla_b_gate_up, gla_out_norm, w_out_c, moe_router,
              moe_w_gate, moe_w_up, moe_w_down):
    y_prompt = trunk(x_prompt, c_prompt, norm_gain, w_ada, b_ada, rel_bias, w_in_ab, mla_q_norm, mla_kv_norm,
                     mla_w_qb, mla_w_kvb, swa_sink, w_out_ab, ffn_w_gate, ffn_w_up, ffn_w_down, w_in_c,
                     gla_w_gate_up, gla_b_gate_up, gla_out_norm, w_out_c, moe_router, moe_w_gate, moe_w_up,
                     moe_w_down)
    y_sample = trunk(x_sample, c_sample, norm_gain, w_ada, b_ada, rel_bias, w_in_ab, mla_q_norm, mla_kv_norm,
                     mla_w_qb, mla_w_kvb, swa_sink, w_out_ab, ffn_w_gate, ffn_w_up, ffn_w_down, w_in_c,
                     gla_w_gate_up, gla_b_gate_up, gla_out_norm, w_out_c, moe_router, moe_w_gate, moe_w_up,
                     moe_w_down)
    return (y_prompt, y_sample)
```

```python
import functools
import math

import jax
import jax.numpy as jnp
from jax import lax
from jax.experimental import pallas as pl
from jax.experimental.pallas import tpu as pltpu

F32 = jnp.float32
BF16 = jnp.bfloat16

MLA_HEADS = 8
MLA_NOPE = 64
MLA_ROPE = 32
MLA_V = 64
MLA_Q_LORA = 384
MLA_KV_LORA = 256
ROPE_THETA = 10000.0
SWA_HEADS = 8
SWA_KV_HEADS = 2
SWA_HEAD_DIM = 64
WINDOW = 128
REL_BUCKETS = 32
REL_MAX_DIST = 128
GLA_HEADS = 4
GLA_GATE_RANK = 16
GLA_GATE_NORM = 16.0
GLA_CHUNK = 64
N_EXPERTS = 8
TOP_K = 2
EPS = 1e-6

LANES = 128
VMEM_LIMIT = 56 * 1024 * 1024


def _cparams(sem, vmem=VMEM_LIMIT):
    return pltpu.CompilerParams(dimension_semantics=sem, vmem_limit_bytes=vmem)


def _tile(n, pref):
    if n <= pref:
        return n
    t = pref
    while n % t:
        t -= 8
    return t


def _rms(x, g):
    return x * lax.rsqrt(jnp.mean(x * x, axis=-1, keepdims=True) + EPS) * g


def _modulate(x, g, shift, scale):
    return _rms(x, g) * (1.0 + scale) + shift


def _silu(x):
    return x / (1.0 + jnp.exp(-x))


def _dot(a, b):
    return jnp.dot(a, b, preferred_element_type=F32)


def _dot_nt(a, b):
    return lax.dot_general(a, b, (((1,), (1,)), ((), ())), preferred_element_type=F32)


def _dot_tn(a, b):
    return lax.dot_general(a, b, (((0,), (0,)), ((), ())), preferred_element_type=F32)


def _ada_kernel(c_ref, w_ref, b_ref, o_ref):
    s = _silu(c_ref[...]).astype(BF16)
    o_ref[...] = _dot(s, w_ref[...].astype(BF16)) + b_ref[...]


def ada_mod(c, w_ada, b_ada, layer):
    B, D = c.shape
    N = w_ada.shape[-1]
    Bp = -(-B // 16) * 16
    cp = jnp.pad(c, ((0, Bp - B), (0, 0)))
    tn = _tile(N, 1536)
    out = pl.pallas_call(
        _ada_kernel,
        grid=(N // tn,),
        in_specs=[pl.BlockSpec((Bp, D), lambda j: (0, 0)),
                  pl.BlockSpec((None, D, tn), lambda j: (layer, 0, j)),
                  pl.BlockSpec((None, 1, tn), lambda j: (layer, 0, j))],
        out_specs=pl.BlockSpec((Bp, tn), lambda j: (0, j)),
        out_shape=jax.ShapeDtypeStruct((Bp, N), F32),
        compiler_params=_cparams(("parallel",)),
        name="ada_mod",
    )(cp, w_ada, b_ada.reshape(b_ada.shape[0], 1, N))
    return out[:B].reshape(B, 6, D)


def _l0_proj_kernel(x_ref, mod_ref, ng_ref, win_ref, qn_ref, kvn_ref, wq_ref, wqs_ref, wk_ref, wv_ref,
                    cq_ref, sq_ref, ck_ref, sk_ref,
                    q_ref, k_ref, v_ref, sq_out_ref, sk_out_ref, sv_out_ref):
    h = _modulate(x_ref[...], ng_ref[0:1, :], mod_ref[0:1, :], mod_ref[1:2, :]).astype(BF16)
    p = _dot(h, win_ref[...])
    o1 = MLA_Q_LORA
    o2 = o1 + MLA_KV_LORA
    nq = _rms(p[:, :o1], qn_ref[...]).astype(BF16)
    q = _dot(nq, wq_ref[...])
    qs = _dot(nq, wqs_ref[...])
    cq = jnp.concatenate([cq_ref[...]] * MLA_HEADS, axis=1)
    sq = jnp.concatenate([sq_ref[...]] * MLA_HEADS, axis=1)
    q_ref[...] = (q * cq + qs * sq).astype(q_ref.dtype)
    nkv = _rms(p[:, o1:o2], kvn_ref[...]).astype(BF16)
    kpe = p[:, o2:o2 + LANES] * ck_ref[...] + p[:, o2 + LANES:o2 + 2 * LANES] * sk_ref[...]
    k = _dot(nkv, wk_ref[...]) + jnp.concatenate([kpe] * MLA_HEADS, axis=1)
    k_ref[...] = k.astype(k_ref.dtype)
    v_ref[...] = _dot(nkv, wv_ref[...]).astype(v_ref.dtype)
    o3 = o2 + 2 * LANES
    o4 = o3 + SWA_HEADS * LANES
    sq_out_ref[...] = (p[:, o3:o4] * (SWA_HEAD_DIM ** -0.5)).astype(sq_out_ref.dtype)
    sk_out_ref[...] = p[:, o4:o4 + LANES].astype(sk_out_ref.dtype)
    sv_out_ref[...] = p[:, o4 + LANES:o4 + 2 * LANES].astype(sv_out_ref.dtype)


def _prep_l0_weights(w_in, w_qb, w_kvb):
    D = w_in.shape[0]
    o1 = MLA_Q_LORA
    o2 = o1 + MLA_KV_LORA
    o2r = o2 + MLA_ROPE
    half = MLA_ROPE // 2
    padl = lambda w, l, r: jnp.pad(w, ((0, 0), (l, r)))
    kpe = w_in[:, o2:o2r]
    kpe_sw = jnp.concatenate([-kpe[:, half:], kpe[:, :half]], axis=1)
    o3 = o2r + SWA_HEADS * SWA_HEAD_DIM
    o4 = o3 + SWA_KV_HEADS * SWA_HEAD_DIM
    g = SWA_HEADS // SWA_KV_HEADS
    swa_q = []
    for h in range(SWA_HEADS):
        hk = h // g
        blk = w_in[:, o2r + h * SWA_HEAD_DIM: o2r + (h + 1) * SWA_HEAD_DIM]
        swa_q.append(padl(blk, hk * SWA_HEAD_DIM, LANES - (hk + 1) * SWA_HEAD_DIM))
    win = jnp.concatenate(
        [w_in[:, :o2], padl(kpe, MLA_NOPE, LANES - MLA_NOPE - MLA_ROPE),
         padl(kpe_sw, MLA_NOPE, LANES - MLA_NOPE - MLA_ROPE)] + swa_q + [w_in[:, o3:o4], w_in[:, o4:]],
        axis=1).astype(BF16)
    dqk = MLA_NOPE + MLA_ROPE
    wq3 = w_qb.reshape(o1, MLA_HEADS, dqk)
    wq = jnp.pad(wq3, ((0, 0), (0, 0), (0, LANES - dqk))).reshape(o1, MLA_HEADS * LANES).astype(BF16)
    x1 = wq3[..., MLA_NOPE:MLA_NOPE + half]
    x2 = wq3[..., MLA_NOPE + half:]
    z = jnp.zeros_like
    wqs = jnp.concatenate([z(wq3[..., :MLA_NOPE]), -x2, x1, z(wq3[..., :LANES - dqk])], axis=-1)
    wqs = wqs.reshape(o1, MLA_HEADS * LANES).astype(BF16)
    wkv3 = w_kvb.reshape(MLA_KV_LORA, MLA_HEADS, MLA_NOPE + MLA_V)
    wk = jnp.pad(wkv3[..., :MLA_NOPE], ((0, 0), (0, 0), (0, LANES - MLA_NOPE)))
    wk = wk.reshape(MLA_KV_LORA, MLA_HEADS * LANES).astype(BF16)
    wv = wkv3[..., MLA_NOPE:].reshape(MLA_KV_LORA, MLA_HEADS * MLA_V).astype(BF16)
    return win, wq, wqs, wk, wv


def _rope_tables(T):
    half = MLA_ROPE // 2
    inv = 1.0 / (ROPE_THETA ** (jnp.arange(half, dtype=F32) / half))
    ang = jnp.arange(T, dtype=F32)[:, None] * inv[None, :]
    cos, sin = jnp.cos(ang), jnp.sin(ang)
    s = (MLA_NOPE + MLA_ROPE) ** -0.5
    one = jnp.ones((T, MLA_NOPE), F32)
    zl = jnp.zeros((T, MLA_NOPE), F32)
    zr = jnp.zeros((T, LANES - MLA_NOPE - MLA_ROPE), F32)
    cq = jnp.concatenate([one * s, cos * s, cos * s, zr], axis=1)
    sq = jnp.concatenate([zl, sin * s, sin * s, zr], axis=1)
    ck = jnp.concatenate([zl, cos, cos, zr], axis=1)
    sk = jnp.concatenate([zl, sin, sin, zr], axis=1)
    return cq, sq, ck, sk


def l0_proj(x, mod, ng, win, qn, kvn, wq, wqs, wk, wv, tables):
    B, T, D = x.shape
    tm = _tile(T, 512)
    NP = win.shape[1]
    HQ = MLA_HEADS * LANES
    full = lambda a: pl.BlockSpec(a.shape, lambda b, i: (0,) * a.ndim)
    row = lambda n: pl.BlockSpec((None, tm, n), lambda b, i: (b, i, 0))
    tab = pl.BlockSpec((tm, LANES), lambda b, i: (i, 0))
    outs = [(HQ, BF16), (HQ, BF16), (MLA_HEADS * MLA_V, BF16), (SWA_HEADS * LANES, BF16), (LANES, BF16),
            (LANES, BF16)]
    return pl.pallas_call(
        _l0_proj_kernel,
        grid=(B, T // tm),
        in_specs=[row(D), pl.BlockSpec((None, 6, D), lambda b, i: (b, 0, 0)), full(ng), full(win), full(qn),
                  full(kvn), full(wq), full(wqs), full(wk), full(wv), tab, tab, tab, tab],
        out_specs=[row(n) for n, _ in outs],
        out_shape=[jax.ShapeDtypeStruct((B, T, n), dt) for n, dt in outs],
        compiler_params=_cparams(("parallel", "parallel")),
        name="l0_proj",
    )(x, mod, ng, win, qn, kvn, wq, wqs, wk, wv, *tables)


def _mla_attn_kernel(q_ref, k_ref, v_ref, o_ref, m_ref, l_ref, acc_ref, *, tk):
    T = k_ref.shape[0]
    nk = T // tk
    m_ref[...] = jnp.full(m_ref.shape, -jnp.inf, F32)
    l_ref[...] = jnp.zeros(l_ref.shape, F32)
    acc_ref[...] = jnp.zeros(acc_ref.shape, F32)

    def body(j, carry):
        r0 = pl.multiple_of(j * tk, tk)
        v = v_ref[pl.ds(r0, tk), :]
        for h in range(2):
            q = q_ref[:, h * LANES:(h + 1) * LANES]
            k = k_ref[pl.ds(r0, tk), h * LANES:(h + 1) * LANES]
            s = _dot_nt(q, k)
            m_prev = m_ref[h]
            m_new = jnp.maximum(m_prev, jnp.max(s, axis=-1, keepdims=True))
            alpha = jnp.exp(m_prev - m_new)
            p = jnp.exp(s - m_new)
            l_ref[h] = alpha * l_ref[h] + jnp.sum(p, axis=-1, keepdims=True)
            acc_ref[h] = alpha * acc_ref[h] + _dot(p.astype(BF16), v)
            m_ref[h] = m_new
        return carry

    lax.fori_loop(0, nk, body, 0)
    o0 = acc_ref[0] / l_ref[0]
    o1 = acc_ref[1] / l_ref[1]
    lane = lax.broadcasted_iota(jnp.int32, o0.shape, 1)
    o_ref[...] = jnp.where(lane < MLA_V, o0, o1).astype(o_ref.dtype)


def mla_attention(q, k, v):
    B, T, _ = q.shape
    tq = _tile(T, 512)
    tk = _tile(T, 512)
    hp = MLA_HEADS // 2
    return pl.pallas_call(
        functools.partial(_mla_attn_kernel, tk=tk),
        grid=(B, hp, T // tq),
        in_specs=[pl.BlockSpec((None, tq, 2 * LANES), lambda b, h, i: (b, i, h)),
                  pl.BlockSpec((None, T, 2 * LANES), lambda b, h, i: (b, 0, h)),
                  pl.BlockSpec((None, T, LANES), lambda b, h, i: (b, 0, h))],
        out_specs=pl.BlockSpec((None, tq, LANES), lambda b, h, i: (b, i, h)),
        out_shape=jax.ShapeDtypeStruct((B, T, MLA_HEADS * MLA_V), BF16),
        scratch_shapes=[pltpu.VMEM((2, tq, 1), F32), pltpu.VMEM((2, tq, 1), F32),
                        pltpu.VMEM((2, tq, LANES), F32)],
        compiler_params=_cparams(("parallel", "parallel", "parallel")),
        name="mla_attention",
    )(q, k, v)


def _t5_bucket(rel):
    nb = REL_BUCKETS // 2
    max_exact = nb // 2
    n = jnp.abs(rel)
    big = max_exact + (jnp.log(jnp.maximum(n, 1).astype(F32) / max_exact)
                       / math.log(REL_MAX_DIST / max_exact) * (nb - max_exact)).astype(jnp.int32)
    big = jnp.minimum(big, nb - 1)
    return jnp.where(rel > 0, nb, 0) + jnp.where(n < max_exact, n, big)


def _swa_bias_kernel(rb_ref, bucket_ref, o_ref):
    h = pl.program_id(0)
    bucket = bucket_ref[...]
    acc = jnp.zeros(bucket.shape, F32)
    for b in range(REL_BUCKETS):
        acc = jnp.where(bucket == b, rb_ref[b * SWA_HEADS + h], acc)
    qi = lax.broadcasted_iota(jnp.int32, bucket.shape, 0)
    kj = lax.broadcasted_iota(jnp.int32, bucket.shape, 1)
    rel = kj - WINDOW - qi
    o_ref[...] = jnp.where(jnp.abs(rel) <= WINDOW, acc, -jnp.inf)


def swa_bias(rel_bias):
    W = WINDOW
    qi = jnp.arange(W, dtype=jnp.int32)[:, None]
    kj = jnp.arange(3 * W, dtype=jnp.int32)[None, :]
    bucket = _t5_bucket(kj - W - qi)
    return pl.pallas_call(
        _swa_bias_kernel,
        grid=(SWA_HEADS,),
        in_specs=[pl.BlockSpec(memory_space=pltpu.SMEM), pl.BlockSpec((W, 3 * W), lambda h: (0, 0))],
        out_specs=pl.BlockSpec((None, W, 3 * W), lambda h: (h, 0, 0)),
        out_shape=jax.ShapeDtypeStruct((SWA_HEADS, W, 3 * W), F32),
        name="swa_bias",
    )(rel_bias.reshape(-1), bucket)


def _swa_kernel(sink_ref, q_ref, kp_ref, kc_ref, kn_ref, vp_ref, vc_ref, vn_ref, bias_ref, o_ref):
    n = pl.program_id(1)
    nb = pl.num_programs(1)
    W = WINDOW
    kband = jnp.concatenate([kp_ref[...], kc_ref[...], kn_ref[...]], axis=0)
    vband = jnp.concatenate([vp_ref[...], vc_ref[...], vn_ref[...]], axis=0)
    col = lax.broadcasted_iota(jnp.int32, (W, 3 * W), 1)
    lo = jnp.where(n == 0, W, 0)
    hi = jnp.where(n == nb - 1, 2 * W, 3 * W)
    inside = (col >= lo) & (col < hi)
    lane = lax.broadcasted_iota(jnp.int32, (W, LANES), 1)
    G = SWA_HEADS // SWA_KV_HEADS
    outs = []
    for g in range(G):
        pair = []
        for hk in range(SWA_KV_HEADS):
            h = hk * G + g
            q = q_ref[:, h * LANES:(h + 1) * LANES]
            s = _dot_nt(q, kband)
            s = jnp.where(inside, s + bias_ref[h], -jnp.inf)
            sink = sink_ref[h]
            m = jnp.maximum(jnp.max(s, axis=-1, keepdims=True), sink)
            p = jnp.exp(s - m)
            den = jnp.sum(p, axis=-1, keepdims=True) + jnp.exp(sink - m)
            pair.append(_dot(p.astype(BF16), vband) / den)
        outs.append(jnp.where(lane < SWA_HEAD_DIM, pair[0], pair[1]))
    o_ref[...] = jnp.concatenate(outs, axis=1).astype(o_ref.dtype)


def swa_attention(q, k, v, sink, bias):
    B, T, _ = q.shape
    W = WINDOW
    nb = T // W
    G = SWA_HEADS // SWA_KV_HEADS
    prev = pl.BlockSpec((None, W, LANES), lambda b, n: (b, jnp.maximum(n - 1, 0), 0))
    cur = pl.BlockSpec((None, W, LANES), lambda b, n: (b, n, 0))
    nxt = pl.BlockSpec((None, W, LANES), lambda b, n: (b, jnp.minimum(n + 1, nb - 1), 0))
    return pl.pallas_call(
        _swa_kernel,
        grid=(B, nb),
        in_specs=[pl.BlockSpec(memory_space=pltpu.SMEM),
                  pl.BlockSpec((None, W, SWA_HEADS * LANES), lambda b, n: (b, n, 0)),
                  prev, cur, nxt, prev, cur, nxt,
                  pl.BlockSpec(bias.shape, lambda b, n: (0, 0, 0))],
        out_specs=pl.BlockSpec((None, W, G * LANES), lambda b, n: (b, n, 0)),
        out_shape=jax.ShapeDtypeStruct((B, T, G * LANES), BF16),
        compiler_params=_cparams(("parallel", "parallel")),
        name="swa_attention",
    )(sink, q, k, k, k, v, v, v, bias)


def _out_res_kernel(a_ref, b_ref, wa_ref, wb_ref, x_ref, mod_ref, ng_ref, o_ref, *, gate_row, gain_row):
    y = _dot(a_ref[...], wa_ref[...]) + _dot(b_ref[...], wb_ref[...])
    o_ref[...] = x_ref[...] + mod_ref[gate_row:gate_row + 1, :] * _rms(y, ng_ref[gain_row:gain_row + 1, :])


def out_res(a, b, wa, wb, x, mod, ng):
    B, T, D = x.shape
    tm = _tile(T, 512)
    full = lambda w: pl.BlockSpec(w.shape, lambda bb, i: (0,) * w.ndim)
    row = lambda n: pl.BlockSpec((None, tm, n), lambda bb, i: (bb, i, 0))
    return pl.pallas_call(
        functools.partial(_out_res_kernel, gate_row=2, gain_row=1),
        grid=(B, T // tm),
        in_specs=[row(a.shape[-1]), row(b.shape[-1]), full(wa), full(wb), row(D),
                  pl.BlockSpec((None, 6, D), lambda bb, i: (bb, 0, 0)), full(ng)],
        out_specs=row(D),
        out_shape=jax.ShapeDtypeStruct((B, T, D), F32),
        compiler_params=_cparams(("parallel", "parallel")),
        name="out_res",
    )(a, b, wa, wb, x, mod, ng)


def _ffn_kernel(x_ref, mod_ref, ng_ref, wg_ref, wu_ref, wd_ref, o_ref, h_ref, acc_ref):
    j = pl.program_id(2)

    @pl.when(j == 0)
    def _():
        h_ref[...] = _modulate(x_ref[...], ng_ref[2:3, :], mod_ref[3:4, :], mod_ref[4:5, :]).astype(BF16)
        acc_ref[...] = jnp.zeros(acc_ref.shape, F32)

    h = h_ref[...]
    a = (_silu(_dot(h, wg_ref[...])) * _dot(h, wu_ref[...])).astype(BF16)
    acc_ref[...] += _dot(a, wd_ref[...])

    @pl.when(j == pl.num_programs(2) - 1)
    def _():
        o_ref[...] = x_ref[...] + mod_ref[5:6, :] * _rms(acc_ref[...], ng_ref[3:4, :])


def ffn(x, mod, ng, wg, wu, wd):
    B, T, D = x.shape
    F = wg.shape[1]
    tm = _tile(T, 512)
    tf = F // 2 if (F // 2) % LANES == 0 else F
    row = pl.BlockSpec((None, tm, D), lambda b, i, j: (b, i, 0))
    return pl.pallas_call(
        _ffn_kernel,
        grid=(B, T // tm, F // tf),
        in_specs=[row, pl.BlockSpec((None, 6, D), lambda b, i, j: (b, 0, 0)),
                  pl.BlockSpec(ng.shape, lambda b, i, j: (0, 0)),
                  pl.BlockSpec((D, tf), lambda b, i, j: (0, j)),
                  pl.BlockSpec((D, tf), lambda b, i, j: (0, j)),
                  pl.BlockSpec((tf, D), lambda b, i, j: (j, 0))],
        out_specs=row,
        out_shape=jax.ShapeDtypeStruct((B, T, D), F32),
        scratch_shapes=[pltpu.VMEM((tm, D), BF16), pltpu.VMEM((tm, D), F32)],
        compiler_params=_cparams(("parallel", "parallel", "arbitrary")),
        name="ffn",
    )(x, mod, ng, wg, wu, wd)


def _l1_proj_kernel(x_ref, mod_ref, ng_ref, win_ref, wgate_ref, bgate_ref,
                    q_ref, k_ref, v_ref, r_ref, gf_ref, gb_ref):
    h = _modulate(x_ref[...], ng_ref[0:1, :], mod_ref[0:1, :], mod_ref[1:2, :]).astype(BF16)
    p = _dot(h, win_ref[...])
    nk = q_ref.shape[-1]
    nv = v_ref.shape[-1]
    dk = nk // GLA_HEADS
    q_ref[...] = p[:, :nk] * (dk ** -0.5)
    k_ref[...] = p[:, nk:2 * nk]
    v_ref[...] = p[:, 2 * nk:2 * nk + nv].astype(v_ref.dtype)
    r_ref[...] = p[:, 2 * nk + nv:2 * nk + 2 * nv].astype(r_ref.dtype)
    gd = p[:, 2 * nk + 2 * nv:].astype(BF16)
    pre = _dot(gd, wgate_ref[...]) + bgate_ref[...]
    logsig = jnp.minimum(pre, 0.0) - jnp.log1p(jnp.exp(-jnp.abs(pre)))
    g = logsig / GLA_GATE_NORM
    gf_ref[...] = g[:, :nk]
    gb_ref[...] = g[:, nk:]


def l1_proj(x, mod, ng, win, wgate, bgate, nk, nv):
    B, T, D = x.shape
    tm = _tile(T, 512)
    full = lambda a: pl.BlockSpec(a.shape, lambda b, i: (0,) * a.ndim)
    row = lambda n: pl.BlockSpec((None, tm, n), lambda b, i: (b, i, 0))
    outs = [(nk, F32), (nk, F32), (nv, BF16), (nv, BF16), (nk, F32), (nk, F32)]
    return pl.pallas_call(
        _l1_proj_kernel,
        grid=(B, T // tm),
        in_specs=[row(D), pl.BlockSpec((None, 6, D), lambda b, i: (b, 0, 0)), full(ng), full(win), full(wgate),
                  full(bgate)],
        out_specs=[row(n) for n, _ in outs],
        out_shape=[jax.ShapeDtypeStruct((B, T, n), dt) for n, dt in outs],
        compiler_params=_cparams(("parallel", "parallel")),
        name="l1_proj",
    )(x, mod, ng, win, wgate, bgate)


def _gla_kernel(q_ref, k_ref, g_ref, v_ref, o_ref, st_ref, *, reverse):
    t = pl.program_id(2)
    L = GLA_CHUNK
    nchunk = q_ref.shape[0] // L

    @pl.when(t == 0)
    def _():
        st_ref[...] = jnp.zeros(st_ref.shape, F32)

    ri = lax.broadcasted_iota(jnp.int32, (L, L), 0)
    ci = lax.broadcasted_iota(jnp.int32, (L, L), 1)
    keep = (ci >= ri) if reverse else (ci <= ri)
    tri = keep.astype(BF16)
    order = range(nchunk - 1, -1, -1) if reverse else range(nchunk)
    for c in order:
        rows = slice(c * L, (c + 1) * L)
        g = g_ref[rows, :]
        g_hi = g.astype(BF16)
        r1 = g - g_hi.astype(F32)
        g_mid = r1.astype(BF16)
        g_lo = (r1 - g_mid.astype(F32)).astype(BF16)
        b = _dot(tri, g_hi) + _dot(tri, g_mid) + _dot(tri, g_lo)
        b_end = b[0:1, :] if reverse else b[L - 1:L, :]
        q = q_ref[rows, :]
        k = k_ref[rows, :]
        v = v_ref[rows, :]
        qe = (q * jnp.exp(b)).astype(BF16)
        ke = (k * jnp.exp(-b)).astype(BF16)
        kd = (k * jnp.exp(b_end - b)).astype(BF16)
        a = jnp.where(keep, _dot_nt(qe, ke), 0.0).astype(BF16)
        st = st_ref[...]
        o_ref[rows, :] = (_dot(a, v) + _dot_nt(qe, st.astype(BF16))).astype(o_ref.dtype)
        st_ref[...] = st * jnp.exp(b_end) + _dot_tn(v, kd)


def gla_scan(q, k, g, v, reverse):
    B, T, nk = q.shape
    nv = v.shape[-1]
    dk, dv = nk // GLA_HEADS, nv // GLA_HEADS
    tc = _tile(T, 512)
    nt = T // tc
    blk = (lambda t: nt - 1 - t) if reverse else (lambda t: t)
    qspec = pl.BlockSpec((None, tc, dk), lambda b, h, t: (b, blk(t), h))
    vspec = pl.BlockSpec((None, tc, dv), lambda b, h, t: (b, blk(t), h))
    return pl.pallas_call(
        functools.partial(_gla_kernel, reverse=reverse),
        grid=(B, GLA_HEADS, nt),
        in_specs=[qspec, qspec, qspec, vspec],
        out_specs=vspec,
        out_shape=jax.ShapeDtypeStruct((B, T, nv), F32),
        scratch_shapes=[pltpu.VMEM((dv, dk), F32)],
        compiler_params=_cparams(("parallel", "parallel", "arbitrary")),
        name="gla_bwd" if reverse else "gla_fwd",
    )(q, k, g, v)


def _gla_out_kernel(of_ref, ob_ref, r_ref, on_ref, w_ref, x_ref, mod_ref, ng_ref, o_ref):
    o = of_ref[...] + ob_ref[...]
    dv = on_ref.shape[-1]
    parts = [_rms(o[:, h * dv:(h + 1) * dv], on_ref[...]) for h in range(GLA_HEADS)]
    on = jnp.concatenate(parts, axis=1)
    y = _dot((on * _silu(r_ref[...].astype(F32))).astype(BF16), w_ref[...])
    o_ref[...] = x_ref[...] + mod_ref[2:3, :] * _rms(y, ng_ref[1:2, :])


def gla_out(o_f, o_b, r, out_norm, w_out, x, mod, ng):
    B, T, D = x.shape
    tm = _tile(T, 512)
    nv = o_f.shape[-1]
    full = lambda w: pl.BlockSpec(w.shape, lambda bb, i: (0,) * w.ndim)
    row = lambda n: pl.BlockSpec((None, tm, n), lambda bb, i: (bb, i, 0))
    return pl.pallas_call(
        _gla_out_kernel,
        grid=(B, T // tm),
        in_specs=[row(nv), row(nv), row(nv), full(out_norm), full(w_out), row(D),
                  pl.BlockSpec((None, 6, D), lambda bb, i: (bb, 0, 0)), full(ng)],
        out_specs=row(D),
        out_shape=jax.ShapeDtypeStruct((B, T, D), F32),
        compiler_params=_cparams(("parallel", "parallel")),
        name="gla_out",
    )(o_f, o_b, r, out_norm, w_out, x, mod, ng)


def _router_kernel(x_ref, mod_ref, ng_ref, whi_ref, wlo_ref, h_ref, idx_ref, gate_ref):
    h = _modulate(x_ref[...], ng_ref[2:3, :], mod_ref[3:4, :], mod_ref[4:5, :])
    h_ref[...] = h
    h_hi = h.astype(BF16)
    h_lo = (h - h_hi.astype(F32)).astype(BF16)
    logits = _dot(h_hi, whi_ref[...]) + _dot(h_lo, whi_ref[...]) + _dot(h_hi, wlo_ref[...])
    lane = lax.broadcasted_iota(jnp.int32, logits.shape, 1)
    logits = jnp.where(lane < N_EXPERTS, logits, -jnp.inf)
    m1 = jnp.max(logits, axis=-1, keepdims=True)
    i1 = jnp.min(jnp.where(logits == m1, lane, LANES), axis=-1, keepdims=True)
    rest = jnp.where(lane == i1, -jnp.inf, logits)
    m2 = jnp.max(rest, axis=-1, keepdims=True)
    i2 = jnp.min(jnp.where(rest == m2, lane, LANES), axis=-1, keepdims=True)
    e = jnp.exp(m2 - m1)
    idx_ref[...] = jnp.concatenate([i1, i2], axis=1)
    gate_ref[...] = jnp.concatenate([1.0 / (1.0 + e), e / (1.0 + e)], axis=1)


def router(x, mod, ng, w_router):
    B, T, D = x.shape
    tm = _tile(T, 512)
    wp = jnp.pad(w_router, ((0, 0), (0, LANES - N_EXPERTS)))
    whi = wp.astype(BF16)
    wlo = (wp - whi.astype(F32)).astype(BF16)
    full = lambda w: pl.BlockSpec(w.shape, lambda bb, i: (0,) * w.ndim)
    row = lambda n: pl.BlockSpec((None, tm, n), lambda bb, i: (bb, i, 0))
    return pl.pallas_call(
        _router_kernel,
        grid=(B, T // tm),
        in_specs=[row(D), pl.BlockSpec((None, 6, D), lambda bb, i: (bb, 0, 0)), full(ng), full(whi), full(wlo)],
        out_specs=[row(D), row(TOP_K), row(TOP_K)],
        out_shape=[jax.ShapeDtypeStruct((B, T, D), F32), jax.ShapeDtypeStruct((B, T, TOP_K), jnp.int32),
                   jax.ShapeDtypeStruct((B, T, TOP_K), F32)],
        compiler_params=_cparams(("parallel", "parallel")),
        name="moe_router",
    )(x, mod, ng, whi, wlo)


def _moe_kernel(blk_e_ref, tok_ref, h_hbm, wg_ref, wu_ref, wd_ref, y_ref, xbuf, xb_ref, acc_ref, sem, *, bm):
    i = pl.program_id(0)
    j = pl.program_id(1)

    def row_copy(r):
        return pltpu.make_async_copy(h_hbm.at[tok_ref[i * bm + r]], xbuf.at[r], sem)

    @pl.when(j == 0)
    def _():
        def start(r, c):
            row_copy(r).start()
            return c

        def wait(r, c):
            row_copy(r).wait()
            return c

        lax.fori_loop(0, bm, start, 0)
        lax.fori_loop(0, bm, wait, 0)
        xb_ref[...] = xbuf[...].astype(BF16)
        acc_ref[...] = jnp.zeros(acc_ref.shape, F32)

    x = xb_ref[...]
    a = (_silu(_dot(x, wg_ref[...])) * _dot(x, wu_ref[...])).astype(BF16)
    acc_ref[...] += _dot(a, wd_ref[...])

    @pl.when(j == pl.num_programs(1) - 1)
    def _():
        y_ref[...] = acc_ref[...]


def moe_experts(h, tok_buf, blk_e, wg, wu, wd, bm):
    N, D = h.shape
    rows = tok_buf.shape[0]
    E, _, F = wg.shape
    tf = F // 7 if F % (7 * LANES) == 0 else F
    return pl.pallas_call(
        functools.partial(_moe_kernel, bm=bm),
        grid_spec=pltpu.PrefetchScalarGridSpec(
            num_scalar_prefetch=2,
            grid=(rows // bm, F // tf),
            in_specs=[pl.BlockSpec(memory_space=pl.ANY),
                      pl.BlockSpec((None, D, tf), lambda i, j, be, tk: (be[i], 0, j)),
                      pl.BlockSpec((None, D, tf), lambda i, j, be, tk: (be[i], 0, j)),
                      pl.BlockSpec((None, tf, D), lambda i, j, be, tk: (be[i], j, 0))],
            out_specs=pl.BlockSpec((bm, D), lambda i, j, be, tk: (i, 0)),
            scratch_shapes=[pltpu.VMEM((bm, D), F32), pltpu.VMEM((bm, D), BF16), pltpu.VMEM((bm, D), F32),
                            pltpu.SemaphoreType.DMA(())],
        ),
        out_shape=jax.ShapeDtypeStruct((rows, D), F32),
        compiler_params=_cparams(("arbitrary", "arbitrary")),
        name="moe_experts",
    )(blk_e, tok_buf, h, wg, wu, wd)


def _moe_combine_kernel(pos_ref, y_hbm, gate_ref, x_ref, mod_ref, ng_ref, o_ref, ybuf, sem, *, tm, rows_per_batch):
    b = pl.program_id(0)
    i = pl.program_id(1)
    base = (b * rows_per_batch + i * tm) * TOP_K

    def row_copy(r, kk):
        return pltpu.make_async_copy(y_hbm.at[pos_ref[base + r * TOP_K + kk]], ybuf.at[kk, r], sem)

    def start(r, c):
        for kk in range(TOP_K):
            row_copy(r, kk).start()
        return c

    def wait(r, c):
        for kk in range(TOP_K):
            row_copy(r, kk).wait()
        return c

    lax.fori_loop(0, tm, start, 0)
    lax.fori_loop(0, tm, wait, 0)
    gate = gate_ref[...]
    y = ybuf[0] * gate[:, 0:1] + ybuf[1] * gate[:, 1:2]
    o_ref[...] = x_ref[...] + mod_ref[5:6, :] * _rms(y, ng_ref[3:4, :])


def moe_combine(pos, y, gates, x, mod, ng):
    B, T, D = x.shape
    tm = _tile(T, 256)
    return pl.pallas_call(
        functools.partial(_moe_combine_kernel, tm=tm, rows_per_batch=T),
        grid_spec=pltpu.PrefetchScalarGridSpec(
            num_scalar_prefetch=1,
            grid=(B, T // tm),
            in_specs=[pl.BlockSpec(memory_space=pl.ANY),
                      pl.BlockSpec((None, tm, TOP_K), lambda b, i, p: (b, i, 0)),
                      pl.BlockSpec((None, tm, D), lambda b, i, p: (b, i, 0)),
                      pl.BlockSpec((None, 6, D), lambda b, i, p: (b, 0, 0)),
                      pl.BlockSpec(ng.shape, lambda b, i, p: (0, 0))],
            out_specs=pl.BlockSpec((None, tm, D), lambda b, i, p: (b, i, 0)),
            scratch_shapes=[pltpu.VMEM((TOP_K, tm, D), F32), pltpu.SemaphoreType.DMA(())],
        ),
        out_shape=jax.ShapeDtypeStruct((B, T, D), F32),
        compiler_params=_cparams(("arbitrary", "arbitrary")),
        name="moe_combine",
    )(pos, y, gates, x, mod, ng)


def _routing_tables(top_idx, bm):
    n = top_idx.shape[0]
    flat_e = top_idx.reshape(-1)
    onehot = (flat_e[:, None] == jnp.arange(N_EXPERTS, dtype=jnp.int32)[None, :]).astype(jnp.int32)
    csum = jnp.cumsum(onehot, axis=0)
    rank = jnp.sum(csum * onehot, axis=1) - 1
    counts = csum[-1]
    padded = (counts + bm - 1) // bm * bm
    pad_end = jnp.cumsum(padded)
    pad_start = pad_end - padded
    pos = jnp.sum(pad_start[None, :] * onehot, axis=1) + rank
    rows = n * TOP_K + N_EXPERTS * bm
    flat_tok = jnp.arange(n * TOP_K, dtype=jnp.int32) // TOP_K
    tok_buf = jnp.zeros((rows,), jnp.int32).at[pos].set(flat_tok)
    blk_start = jnp.arange(rows // bm, dtype=jnp.int32) * bm
    blk_e = jnp.minimum(jnp.searchsorted(pad_end, blk_start, side='right'), N_EXPERTS - 1).astype(jnp.int32)
    return pos.astype(jnp.int32), tok_buf, blk_e


def moe(x, mod, ng, w_router, wg, wu, wd, bm):
    B, T, D = x.shape
    h, top_idx, gates = router(x, mod, ng, w_router)
    pos, tok_buf, blk_e = _routing_tables(top_idx.reshape(B * T, TOP_K), bm)
    y = moe_experts(h.reshape(B * T, D), tok_buf, blk_e, wg, wu, wd, bm)
    return moe_combine(pos, y, gates, x, mod, ng)


def _prep_weights(p):
    w = {}
    w['l0'] = _prep_l0_weights(p['w_in_ab'][0], p['mla_w_qb'][0], p['mla_w_kvb'][0])
    w['qn'] = p['mla_q_norm'][0][None, :]
    w['kvn'] = p['mla_kv_norm'][0][None, :]
    w_out = p['w_out_ab'][0]
    na = MLA_HEADS * MLA_V
    g = SWA_HEADS // SWA_KV_HEADS
    D = w_out.shape[1]
    w['wa'] = w_out[:na].astype(BF16)
    w['wb'] = w_out[na:].reshape(SWA_KV_HEADS, g, SWA_HEAD_DIM, D).transpose(1, 0, 2, 3).reshape(-1, D).astype(BF16)
    w['ffn'] = tuple(p[k][0].astype(BF16) for k in ('ffn_w_gate', 'ffn_w_up', 'ffn_w_down'))
    w_in_c = p['w_in_c'][0]
    ncol = w_in_c.shape[1]
    w['win_c'] = jnp.pad(w_in_c, ((0, 0), (0, LANES - 2 * GLA_GATE_RANK))).astype(BF16)
    wgu = p['gla_w_gate_up'][0]
    nk = wgu.shape[-1]
    zero = jnp.zeros((GLA_GATE_RANK, nk), F32)
    wgate = jnp.concatenate([jnp.concatenate([wgu[0], zero], axis=1), jnp.concatenate([zero, wgu[1]], axis=1)], axis=0)
    w['wgate'] = jnp.pad(wgate, ((0, LANES - 2 * GLA_GATE_RANK), (0, 0))).astype(BF16)
    w['bgate'] = p['gla_b_gate_up'][0].reshape(1, 2 * nk)
    w['nk'] = nk
    w['nv'] = (ncol - 2 * nk - 2 * GLA_GATE_RANK) // 2
    w['out_norm'] = p['gla_out_norm'][0][None, :]
    w['w_out_c'] = p['w_out_c'][0].astype(BF16)
    w['router'] = p['moe_router'][0]
    w['moe'] = tuple(p[k][0].astype(BF16) for k in ('moe_w_gate', 'moe_w_up', 'moe_w_down'))
    w['swa_bias'] = swa_bias(p['rel_bias'])
    w['sink'] = p['swa_sink'][0]
    return w


def _trunk(x, c, p, w, moe_bm):
    B, T, D = x.shape
    tables = _rope_tables(T)
    mod = ada_mod(c, p['w_ada'], p['b_ada'], 0)
    ng = p['norm_gain'][0]
    q, k, v, sq, sk, sv = l0_proj(x, mod, ng, w['l0'][0], w['qn'], w['kvn'], *w['l0'][1:], tables)
    a = mla_attention(q, k, v)
    b = swa_attention(sq, sk, sv, w['sink'], w['swa_bias'])
    x = out_res(a, b, w['wa'], w['wb'], x, mod, ng)
    x = ffn(x, mod, ng, *w['ffn'])
    mod = ada_mod(c, p['w_ada'], p['b_ada'], 1)
    ng = p['norm_gain'][1]
    q, k, v, r, gf, gb = l1_proj(x, mod, ng, w['win_c'], w['wgate'], w['bgate'], w['nk'], w['nv'])
    o_f = gla_scan(q, k, gf, v, reverse=False)
    o_b = gla_scan(q, k, gb, v, reverse=True)
    x = gla_out(o_f, o_b, r, w['out_norm'], w['w_out_c'], x, mod, ng)
    return moe(x, mod, ng, w['router'], *w['moe'], moe_bm)


def kernel(x_prompt, x_sample, c_prompt, c_sample, norm_gain, w_ada, b_ada, rel_bias, w_in_ab, mla_q_norm, mla_kv_norm, mla_w_qb, mla_w_kvb, swa_sink, w_out_ab, ffn_w_gate, ffn_w_up, ffn_w_down, w_in_c, gla_w_gate_up, gla_b_gate_up, gla_out_norm, w_out_c, moe_router, moe_w_gate, moe_w_up, moe_w_down):
    p = dict(norm_gain=norm_gain, w_ada=w_ada, b_ada=b_ada, rel_bias=rel_bias, w_in_ab=w_in_ab,
             mla_q_norm=mla_q_norm, mla_kv_norm=mla_kv_norm, mla_w_qb=mla_w_qb, mla_w_kvb=mla_w_kvb,
             swa_sink=swa_sink, w_out_ab=w_out_ab, ffn_w_gate=ffn_w_gate, ffn_w_up=ffn_w_up,
             ffn_w_down=ffn_w_down, w_in_c=w_in_c, gla_w_gate_up=gla_w_gate_up, gla_b_gate_up=gla_b_gate_up,
             gla_out_norm=gla_out_norm, w_out_c=w_out_c, moe_router=moe_router, moe_w_gate=moe_w_gate,
             moe_w_up=moe_w_up, moe_w_down=moe_w_down)
    w = _prep_weights(p)
    bm = 512
    return (_trunk(x_prompt, c_prompt, p, w, bm), _trunk(x_sample, c_sample, p, w, bm))
```

```python
import functools
import math

import jax
import jax.numpy as jnp
from jax import lax
from jax.experimental import pallas as pl
from jax.experimental.pallas import tpu as pltpu

F32 = jnp.float32
BF16 = jnp.bfloat16

MLA_HEADS = 8
MLA_NOPE = 64
MLA_ROPE = 32
MLA_V = 64
MLA_Q_LORA = 384
MLA_KV_LORA = 256
ROPE_THETA = 10000.0
SWA_HEADS = 8
SWA_KV_HEADS = 2
SWA_HEAD_DIM = 64
WINDOW = 128
REL_BUCKETS = 32
REL_MAX_DIST = 128
GLA_HEADS = 4
GLA_GATE_RANK = 16
GLA_GATE_NORM = 16.0
GLA_CHUNK = 64
N_EXPERTS = 8
TOP_K = 2
EPS = 1e-6

LANES = 128
VMEM_LIMIT = 56 * 1024 * 1024


def _cparams(sem, vmem=VMEM_LIMIT):
    return pltpu.CompilerParams(dimension_semantics=sem, vmem_limit_bytes=vmem)


def _tile(n, pref):
    if n <= pref:
        return n
    t = pref
    while n % t:
        t -= 8
    return t


def _rms(x, g):
    return x * lax.rsqrt(jnp.mean(x * x, axis=-1, keepdims=True) + EPS) * g


def _modulate(x, g, shift, scale):
    return _rms(x, g) * (1.0 + scale) + shift


def _silu(x):
    return x / (1.0 + jnp.exp(-x))


def _dot(a, b):
    return jnp.dot(a, b, preferred_element_type=F32)


def _dot_nt(a, b):
    return lax.dot_general(a, b, (((1,), (1,)), ((), ())), preferred_element_type=F32)


def _dot_tn(a, b):
    return lax.dot_general(a, b, (((0,), (0,)), ((), ())), preferred_element_type=F32)


def _ada_kernel(c_ref, w_ref, b_ref, o_ref):
    s = _silu(c_ref[...]).astype(BF16)
    o_ref[...] = _dot(s, w_ref[...].astype(BF16)) + b_ref[...]


def ada_mod(c, w_ada, b_ada, layer):
    B, D = c.shape
    N = w_ada.shape[-1]
    Bp = -(-B // 16) * 16
    cp = jnp.pad(c, ((0, Bp - B), (0, 0)))
    tn = _tile(N, 1536)
    out = pl.pallas_call(
        _ada_kernel,
        grid=(N // tn,),
        in_specs=[pl.BlockSpec((Bp, D), lambda j: (0, 0)),
                  pl.BlockSpec((None, D, tn), lambda j: (layer, 0, j)),
                  pl.BlockSpec((None, 1, tn), lambda j: (layer, 0, j))],
        out_specs=pl.BlockSpec((Bp, tn), lambda j: (0, j)),
        out_shape=jax.ShapeDtypeStruct((Bp, N), F32),
        compiler_params=_cparams(("parallel",)),
        name="ada_mod",
    )(cp, w_ada, b_ada.reshape(b_ada.shape[0], 1, N))
    return out[:B].reshape(B, 6, D)


def _l0_proj_kernel(x_ref, mod_ref, ng_ref, win_ref, qn_ref, kvn_ref, wq_ref, wqs_ref, wk_ref, wv_ref,
                    cq_ref, sq_ref, ck_ref, sk_ref,
                    q_ref, k_ref, v_ref, sq_out_ref, sk_out_ref, sv_out_ref):
    h = _modulate(x_ref[...], ng_ref[0:1, :], mod_ref[0:1, :], mod_ref[1:2, :]).astype(BF16)
    p = _dot(h, win_ref[...])
    o1 = MLA_Q_LORA
    o2 = o1 + MLA_KV_LORA
    nq = _rms(p[:, :o1], qn_ref[...]).astype(BF16)
    q = _dot(nq, wq_ref[...])
    qs = _dot(nq, wqs_ref[...])
    cq = jnp.concatenate([cq_ref[...]] * MLA_HEADS, axis=1)
    sq = jnp.concatenate([sq_ref[...]] * MLA_HEADS, axis=1)
    q_ref[...] = (q * cq + qs * sq).astype(q_ref.dtype)
    nkv = _rms(p[:, o1:o2], kvn_ref[...]).astype(BF16)
    kpe = p[:, o2:o2 + LANES] * ck_ref[...] + p[:, o2 + LANES:o2 + 2 * LANES] * sk_ref[...]
    k = _dot(nkv, wk_ref[...]) + jnp.concatenate([kpe] * MLA_HEADS, axis=1)
    k_ref[...] = k.astype(k_ref.dtype)
    v = _dot(nkv, wv_ref[...])
    lane = lax.broadcasted_iota(jnp.int32, v.shape, 1)
    v_ref[...] = jnp.where((lane & (LANES - 1)) == MLA_V, 1.0, v).astype(v_ref.dtype)
    o3 = o2 + 2 * LANES
    o4 = o3 + SWA_HEADS * LANES
    sq_out_ref[...] = (p[:, o3:o4] * (SWA_HEAD_DIM ** -0.5)).astype(sq_out_ref.dtype)
    sk_out_ref[...] = p[:, o4:o4 + LANES].astype(sk_out_ref.dtype)
    sv_out_ref[...] = p[:, o4 + LANES:o4 + 2 * LANES].astype(sv_out_ref.dtype)


def _prep_l0_weights(w_in, w_qb, w_kvb):
    D = w_in.shape[0]
    o1 = MLA_Q_LORA
    o2 = o1 + MLA_KV_LORA
    o2r = o2 + MLA_ROPE
    half = MLA_ROPE // 2
    padl = lambda w, l, r: jnp.pad(w, ((0, 0), (l, r)))
    kpe = w_in[:, o2:o2r]
    kpe_sw = jnp.concatenate([-kpe[:, half:], kpe[:, :half]], axis=1)
    o3 = o2r + SWA_HEADS * SWA_HEAD_DIM
    o4 = o3 + SWA_KV_HEADS * SWA_HEAD_DIM
    g = SWA_HEADS // SWA_KV_HEADS
    swa_q = []
    for h in range(SWA_HEADS):
        hk = h // g
        blk = w_in[:, o2r + h * SWA_HEAD_DIM: o2r + (h + 1) * SWA_HEAD_DIM]
        swa_q.append(padl(blk, hk * SWA_HEAD_DIM, LANES - (hk + 1) * SWA_HEAD_DIM))
    win = jnp.concatenate(
        [w_in[:, :o2], padl(kpe, MLA_NOPE, LANES - MLA_NOPE - MLA_ROPE),
         padl(kpe_sw, MLA_NOPE, LANES - MLA_NOPE - MLA_ROPE)] + swa_q + [w_in[:, o3:o4], w_in[:, o4:]],
        axis=1).astype(BF16)
    dqk = MLA_NOPE + MLA_ROPE
    wq3 = w_qb.reshape(o1, MLA_HEADS, dqk)
    wq = jnp.pad(wq3, ((0, 0), (0, 0), (0, LANES - dqk))).reshape(o1, MLA_HEADS * LANES).astype(BF16)
    x1 = wq3[..., MLA_NOPE:MLA_NOPE + half]
    x2 = wq3[..., MLA_NOPE + half:]
    z = jnp.zeros_like
    wqs = jnp.concatenate([z(wq3[..., :MLA_NOPE]), -x2, x1, z(wq3[..., :LANES - dqk])], axis=-1)
    wqs = wqs.reshape(o1, MLA_HEADS * LANES).astype(BF16)
    wkv3 = w_kvb.reshape(MLA_KV_LORA, MLA_HEADS, MLA_NOPE + MLA_V)
    wk = jnp.pad(wkv3[..., :MLA_NOPE], ((0, 0), (0, 0), (0, LANES - MLA_NOPE)))
    wk = wk.reshape(MLA_KV_LORA, MLA_HEADS * LANES).astype(BF16)
    wv = jnp.pad(wkv3[..., MLA_NOPE:], ((0, 0), (0, 0), (0, LANES - MLA_V)))
    wv = wv.reshape(MLA_KV_LORA, MLA_HEADS * LANES).astype(BF16)
    return win, wq, wqs, wk, wv


def _rope_tables(T):
    half = MLA_ROPE // 2
    inv = 1.0 / (ROPE_THETA ** (jnp.arange(half, dtype=F32) / half))
    ang = jnp.arange(T, dtype=F32)[:, None] * inv[None, :]
    cos, sin = jnp.cos(ang), jnp.sin(ang)
    s = (MLA_NOPE + MLA_ROPE) ** -0.5 * math.log2(math.e)
    one = jnp.ones((T, MLA_NOPE), F32)
    zl = jnp.zeros((T, MLA_NOPE), F32)
    zr = jnp.zeros((T, LANES - MLA_NOPE - MLA_ROPE), F32)
    cq = jnp.concatenate([one * s, cos * s, cos * s, zr], axis=1)
    sq = jnp.concatenate([zl, sin * s, sin * s, zr], axis=1)
    ck = jnp.concatenate([zl, cos, cos, zr], axis=1)
    sk = jnp.concatenate([zl, sin, sin, zr], axis=1)
    return cq, sq, ck, sk


def l0_proj(x, mod, ng, win, qn, kvn, wq, wqs, wk, wv, tables):
    B, T, D = x.shape
    tm = _tile(T, 512)
    NP = win.shape[1]
    HQ = MLA_HEADS * LANES
    full = lambda a: pl.BlockSpec(a.shape, lambda b, i: (0,) * a.ndim)
    row = lambda n: pl.BlockSpec((None, tm, n), lambda b, i: (b, i, 0))
    tab = pl.BlockSpec((tm, LANES), lambda b, i: (i, 0))
    outs = [(HQ, BF16), (HQ, BF16), (HQ, BF16), (SWA_HEADS * LANES, BF16), (LANES, BF16), (LANES, BF16)]
    return pl.pallas_call(
        _l0_proj_kernel,
        grid=(B, T // tm),
        in_specs=[row(D), pl.BlockSpec((None, 6, D), lambda b, i: (b, 0, 0)), full(ng), full(win), full(qn),
                  full(kvn), full(wq), full(wqs), full(wk), full(wv), tab, tab, tab, tab],
        out_specs=[row(n) for n, _ in outs],
        out_shape=[jax.ShapeDtypeStruct((B, T, n), dt) for n, dt in outs],
        compiler_params=_cparams(("parallel", "parallel")),
        name="l0_proj",
    )(x, mod, ng, win, qn, kvn, wq, wqs, wk, wv, *tables)


def _mla_attn_kernel(q_ref, k_ref, v_ref, o_ref, m_ref, acc_ref, *, tk):
    T = k_ref.shape[0]
    nk = T // tk
    m_ref[...] = jnp.full(m_ref.shape, -jnp.inf, F32)
    acc_ref[...] = jnp.zeros(acc_ref.shape, F32)
    q = q_ref[...]

    def body(j, carry):
        r0 = pl.multiple_of(j * tk, tk)
        s = _dot_nt(q, k_ref[pl.ds(r0, tk), :])
        m_prev = m_ref[...]
        m_new = jnp.maximum(m_prev, jnp.max(s, axis=-1, keepdims=True))
        p = jnp.exp2(s - jnp.tile(m_new, (1, tk // LANES)))
        acc_ref[...] = jnp.exp2(m_prev - m_new) * acc_ref[...] + _dot(p.astype(BF16), v_ref[pl.ds(r0, tk), :])
        m_ref[...] = m_new
        return carry

    lax.fori_loop(0, nk, body, 0, unroll=2 if nk % 2 == 0 else 1)
    acc = acc_ref[...]
    o_ref[...] = (acc / acc[:, MLA_V:MLA_V + 1]).astype(o_ref.dtype)


def mla_attention(q, k, v):
    B, T, _ = q.shape
    tq = _tile(T, 512)
    tk = _tile(T, 1024)
    qspec = pl.BlockSpec((None, tq, LANES), lambda b, h, i: (b, i, h))
    kspec = pl.BlockSpec((None, T, LANES), lambda b, h, i: (b, 0, h))
    return pl.pallas_call(
        functools.partial(_mla_attn_kernel, tk=tk),
        grid=(B, MLA_HEADS, T // tq),
        in_specs=[qspec, kspec, kspec],
        out_specs=qspec,
        out_shape=jax.ShapeDtypeStruct((B, T, MLA_HEADS * LANES), BF16),
        scratch_shapes=[pltpu.VMEM((tq, LANES), F32), pltpu.VMEM((tq, LANES), F32)],
        compiler_params=_cparams(("parallel", "parallel", "parallel")),
        name="mla_attention",
    )(q, k, v)


def _t5_bucket(rel):
    nb = REL_BUCKETS // 2
    max_exact = nb // 2
    n = jnp.abs(rel)
    big = max_exact + (jnp.log(jnp.maximum(n, 1).astype(F32) / max_exact)
                       / math.log(REL_MAX_DIST / max_exact) * (nb - max_exact)).astype(jnp.int32)
    big = jnp.minimum(big, nb - 1)
    return jnp.where(rel > 0, nb, 0) + jnp.where(n < max_exact, n, big)


def _swa_bias_kernel(rb_ref, bucket_ref, o_ref):
    h = pl.program_id(0)
    bucket = bucket_ref[...]
    acc = jnp.zeros(bucket.shape, F32)
    for b in range(REL_BUCKETS):
        acc = jnp.where(bucket == b, rb_ref[b * SWA_HEADS + h], acc)
    qi = lax.broadcasted_iota(jnp.int32, bucket.shape, 0)
    kj = lax.broadcasted_iota(jnp.int32, bucket.shape, 1)
    rel = kj - WINDOW - qi
    o_ref[...] = jnp.where(jnp.abs(rel) <= WINDOW, acc, -jnp.inf)


def swa_bias(rel_bias):
    W = WINDOW
    qi = jnp.arange(W, dtype=jnp.int32)[:, None]
    kj = jnp.arange(3 * W, dtype=jnp.int32)[None, :]
    bucket = _t5_bucket(kj - W - qi)
    return pl.pallas_call(
        _swa_bias_kernel,
        grid=(SWA_HEADS,),
        in_specs=[pl.BlockSpec(memory_space=pltpu.SMEM), pl.BlockSpec((W, 3 * W), lambda h: (0, 0))],
        out_specs=pl.BlockSpec((None, W, 3 * W), lambda h: (h, 0, 0)),
        out_shape=jax.ShapeDtypeStruct((SWA_HEADS, W, 3 * W), F32),
        name="swa_bias",
    )(rel_bias.reshape(-1), bucket)


def _swa_kernel(sink_ref, q_ref, kp_ref, kc_ref, kn_ref, vp_ref, vc_ref, vn_ref, bias_ref, o_ref):
    n = pl.program_id(1)
    nb = pl.num_programs(1)
    W = WINDOW
    kband = jnp.concatenate([kp_ref[...], kc_ref[...], kn_ref[...]], axis=0)
    vband = jnp.concatenate([vp_ref[...], vc_ref[...], vn_ref[...]], axis=0)
    col = lax.broadcasted_iota(jnp.int32, (W, 3 * W), 1)
    lo = jnp.where(n == 0, W, 0)
    hi = jnp.where(n == nb - 1, 2 * W, 3 * W)
    inside = (col >= lo) & (col < hi)
    lane = lax.broadcasted_iota(jnp.int32, (W, LANES), 1)
    G = SWA_HEADS // SWA_KV_HEADS
    outs = []
    for g in range(G):
        pair = []
        for hk in range(SWA_KV_HEADS):
            h = hk * G + g
            q = q_ref[:, h * LANES:(h + 1) * LANES]
            s = _dot_nt(q, kband)
            s = jnp.where(inside, s + bias_ref[h], -jnp.inf)
            sink = sink_ref[h]
            m = jnp.maximum(jnp.max(s, axis=-1, keepdims=True), sink)
            p = jnp.exp(s - m)
            den = jnp.sum(p, axis=-1, keepdims=True) + jnp.exp(sink - m)
            pair.append(_dot(p.astype(BF16), vband) / den)
        outs.append(jnp.where(lane < SWA_HEAD_DIM, pair[0], pair[1]))
    o_ref[...] = jnp.concatenate(outs, axis=1).astype(o_ref.dtype)


def swa_attention(q, k, v, sink, bias):
    B, T, _ = q.shape
    W = WINDOW
    nb = T // W
    G = SWA_HEADS // SWA_KV_HEADS
    prev = pl.BlockSpec((None, W, LANES), lambda b, n: (b, jnp.maximum(n - 1, 0), 0))
    cur = pl.BlockSpec((None, W, LANES), lambda b, n: (b, n, 0))
    nxt = pl.BlockSpec((None, W, LANES), lambda b, n: (b, jnp.minimum(n + 1, nb - 1), 0))
    return pl.pallas_call(
        _swa_kernel,
        grid=(B, nb),
        in_specs=[pl.BlockSpec(memory_space=pltpu.SMEM),
                  pl.BlockSpec((None, W, SWA_HEADS * LANES), lambda b, n: (b, n, 0)),
                  prev, cur, nxt, prev, cur, nxt,
                  pl.BlockSpec(bias.shape, lambda b, n: (0, 0, 0))],
        out_specs=pl.BlockSpec((None, W, G * LANES), lambda b, n: (b, n, 0)),
        out_shape=jax.ShapeDtypeStruct((B, T, G * LANES), BF16),
        compiler_params=_cparams(("parallel", "parallel")),
        name="swa_attention",
    )(sink, q, k, k, k, v, v, v, bias)


def _out_res_kernel(a_ref, b_ref, wa_ref, wb_ref, x_ref, mod_ref, ng_ref, o_ref, *, gate_row, gain_row):
    y = _dot(a_ref[...], wa_ref[...]) + _dot(b_ref[...], wb_ref[...])
    o_ref[...] = x_ref[...] + mod_ref[gate_row:gate_row + 1, :] * _rms(y, ng_ref[gain_row:gain_row + 1, :])


def out_res(a, b, wa, wb, x, mod, ng):
    B, T, D = x.shape
    tm = _tile(T, 512)
    full = lambda w: pl.BlockSpec(w.shape, lambda bb, i: (0,) * w.ndim)
    row = lambda n: pl.BlockSpec((None, tm, n), lambda bb, i: (bb, i, 0))
    return pl.pallas_call(
        functools.partial(_out_res_kernel, gate_row=2, gain_row=1),
        grid=(B, T // tm),
        in_specs=[row(a.shape[-1]), row(b.shape[-1]), full(wa), full(wb), row(D),
                  pl.BlockSpec((None, 6, D), lambda bb, i: (bb, 0, 0)), full(ng)],
        out_specs=row(D),
        out_shape=jax.ShapeDtypeStruct((B, T, D), F32),
        compiler_params=_cparams(("parallel", "parallel")),
        name="out_res",
    )(a, b, wa, wb, x, mod, ng)


def _ffn_kernel(x_ref, mod_ref, ng_ref, wg_ref, wu_ref, wd_ref, o_ref, h_ref, acc_ref):
    j = pl.program_id(2)

    @pl.when(j == 0)
    def _():
        h_ref[...] = _modulate(x_ref[...], ng_ref[2:3, :], mod_ref[3:4, :], mod_ref[4:5, :]).astype(BF16)
        acc_ref[...] = jnp.zeros(acc_ref.shape, F32)

    h = h_ref[...]
    a = (_silu(_dot(h, wg_ref[...])) * _dot(h, wu_ref[...])).astype(BF16)
    acc_ref[...] += _dot(a, wd_ref[...])

    @pl.when(j == pl.num_programs(2) - 1)
    def _():
        o_ref[...] = x_ref[...] + mod_ref[5:6, :] * _rms(acc_ref[...], ng_ref[3:4, :])


def ffn(x, mod, ng, wg, wu, wd):
    B, T, D = x.shape
    F = wg.shape[1]
    tm = _tile(T, 512)
    tf = F // 2 if (F // 2) % LANES == 0 else F
    row = pl.BlockSpec((None, tm, D), lambda b, i, j: (b, i, 0))
    return pl.pallas_call(
        _ffn_kernel,
        grid=(B, T // tm, F // tf),
        in_specs=[row, pl.BlockSpec((None, 6, D), lambda b, i, j: (b, 0, 0)),
                  pl.BlockSpec(ng.shape, lambda b, i, j: (0, 0)),
                  pl.BlockSpec((D, tf), lambda b, i, j: (0, j)),
                  pl.BlockSpec((D, tf), lambda b, i, j: (0, j)),
                  pl.BlockSpec((tf, D), lambda b, i, j: (j, 0))],
        out_specs=row,
        out_shape=jax.ShapeDtypeStruct((B, T, D), F32),
        scratch_shapes=[pltpu.VMEM((tm, D), BF16), pltpu.VMEM((tm, D), F32)],
        compiler_params=_cparams(("parallel", "parallel", "arbitrary")),
        name="ffn",
    )(x, mod, ng, wg, wu, wd)


def _l1_proj_kernel(x_ref, mod_ref, ng_ref, win_ref, wgate_ref, bgate_ref,
                    q_ref, k_ref, v_ref, r_ref, gf_ref, gb_ref):
    h = _modulate(x_ref[...], ng_ref[0:1, :], mod_ref[0:1, :], mod_ref[1:2, :]).astype(BF16)
    p = _dot(h, win_ref[...])
    nk = q_ref.shape[-1]
    nv = v_ref.shape[-1]
    dk = nk // GLA_HEADS
    q_ref[...] = p[:, :nk] * (dk ** -0.5)
    k_ref[...] = p[:, nk:2 * nk]
    v_ref[...] = p[:, 2 * nk:2 * nk + nv].astype(v_ref.dtype)
    r_ref[...] = p[:, 2 * nk + nv:2 * nk + 2 * nv].astype(r_ref.dtype)
    gd = p[:, 2 * nk + 2 * nv:].astype(BF16)
    pre = _dot(gd, wgate_ref[...]) + bgate_ref[...]
    logsig = jnp.minimum(pre, 0.0) - jnp.log1p(jnp.exp(-jnp.abs(pre)))
    g = logsig / GLA_GATE_NORM
    gf_ref[...] = g[:, :nk]
    gb_ref[...] = g[:, nk:]


def l1_proj(x, mod, ng, win, wgate, bgate, nk, nv):
    B, T, D = x.shape
    tm = _tile(T, 512)
    full = lambda a: pl.BlockSpec(a.shape, lambda b, i: (0,) * a.ndim)
    row = lambda n: pl.BlockSpec((None, tm, n), lambda b, i: (b, i, 0))
    outs = [(nk, F32), (nk, F32), (nv, BF16), (nv, BF16), (nk, F32), (nk, F32)]
    return pl.pallas_call(
        _l1_proj_kernel,
        grid=(B, T // tm),
        in_specs=[row(D), pl.BlockSpec((None, 6, D), lambda b, i: (b, 0, 0)), full(ng), full(win), full(wgate),
                  full(bgate)],
        out_specs=[row(n) for n, _ in outs],
        out_shape=[jax.ShapeDtypeStruct((B, T, n), dt) for n, dt in outs],
        compiler_params=_cparams(("parallel", "parallel")),
        name="l1_proj",
    )(x, mod, ng, win, wgate, bgate)


def _gla_kernel(q_ref, k_ref, g_ref, v_ref, o_ref, st_ref, *, reverse):
    t = pl.program_id(2)
    L = GLA_CHUNK
    nchunk = q_ref.shape[0] // L

    @pl.when(t == 0)
    def _():
        st_ref[...] = jnp.zeros(st_ref.shape, F32)

    ri = lax.broadcasted_iota(jnp.int32, (L, L), 0)
    ci = lax.broadcasted_iota(jnp.int32, (L, L), 1)
    keep = (ci >= ri) if reverse else (ci <= ri)
    tri = keep.astype(BF16)
    order = range(nchunk - 1, -1, -1) if reverse else range(nchunk)
    for c in order:
        rows = slice(c * L, (c + 1) * L)
        g = g_ref[rows, :]
        g_hi = g.astype(BF16)
        r1 = g - g_hi.astype(F32)
        g_mid = r1.astype(BF16)
        g_lo = (r1 - g_mid.astype(F32)).astype(BF16)
        b = _dot(tri, g_hi) + _dot(tri, g_mid) + _dot(tri, g_lo)
        b_end = b[0:1, :] if reverse else b[L - 1:L, :]
        q = q_ref[rows, :]
        k = k_ref[rows, :]
        v = v_ref[rows, :]
        qe = (q * jnp.exp(b)).astype(BF16)
        ke = (k * jnp.exp(-b)).astype(BF16)
        kd = (k * jnp.exp(b_end - b)).astype(BF16)
        a = jnp.where(keep, _dot_nt(qe, ke), 0.0).astype(BF16)
        st = st_ref[...]
        o_ref[rows, :] = (_dot(a, v) + _dot_nt(qe, st.astype(BF16))).astype(o_ref.dtype)
        st_ref[...] = st * jnp.exp(b_end) + _dot_tn(v, kd)


def gla_scan(q, k, g, v, reverse):
    B, T, nk = q.shape
    nv = v.shape[-1]
    dk, dv = nk // GLA_HEADS, nv // GLA_HEADS
    tc = _tile(T, 512)
    nt = T // tc
    blk = (lambda t: nt - 1 - t) if reverse else (lambda t: t)
    qspec = pl.BlockSpec((None, tc, dk), lambda b, h, t: (b, blk(t), h))
    vspec = pl.BlockSpec((None, tc, dv), lambda b, h, t: (b, blk(t), h))
    return pl.pallas_call(
        functools.partial(_gla_kernel, reverse=reverse),
        grid=(B, GLA_HEADS, nt),
        in_specs=[qspec, qspec, qspec, vspec],
        out_specs=vspec,
        out_shape=jax.ShapeDtypeStruct((B, T, nv), F32),
        scratch_shapes=[pltpu.VMEM((dv, dk), F32)],
        compiler_params=_cparams(("parallel", "parallel", "arbitrary")),
        name="gla_bwd" if reverse else "gla_fwd",
    )(q, k, g, v)


def _gla_out_kernel(of_ref, ob_ref, r_ref, on_ref, w_ref, x_ref, mod_ref, ng_ref, o_ref):
    o = of_ref[...] + ob_ref[...]
    dv = on_ref.shape[-1]
    parts = [_rms(o[:, h * dv:(h + 1) * dv], on_ref[...]) for h in range(GLA_HEADS)]
    on = jnp.concatenate(parts, axis=1)
    y = _dot((on * _silu(r_ref[...].astype(F32))).astype(BF16), w_ref[...])
    o_ref[...] = x_ref[...] + mod_ref[2:3, :] * _rms(y, ng_ref[1:2, :])


def gla_out(o_f, o_b, r, out_norm, w_out, x, mod, ng):
    B, T, D = x.shape
    tm = _tile(T, 512)
    nv = o_f.shape[-1]
    full = lambda w: pl.BlockSpec(w.shape, lambda bb, i: (0,) * w.ndim)
    row = lambda n: pl.BlockSpec((None, tm, n), lambda bb, i: (bb, i, 0))
    return pl.pallas_call(
        _gla_out_kernel,
        grid=(B, T // tm),
        in_specs=[row(nv), row(nv), row(nv), full(out_norm), full(w_out), row(D),
                  pl.BlockSpec((None, 6, D), lambda bb, i: (bb, 0, 0)), full(ng)],
        out_specs=row(D),
        out_shape=jax.ShapeDtypeStruct((B, T, D), F32),
        compiler_params=_cparams(("parallel", "parallel")),
        name="gla_out",
    )(o_f, o_b, r, out_norm, w_out, x, mod, ng)


def _router_kernel(x_ref, mod_ref, ng_ref, whi_ref, wlo_ref, h_ref, idx_ref, gate_ref):
    h = _modulate(x_ref[...], ng_ref[2:3, :], mod_ref[3:4, :], mod_ref[4:5, :])
    h_ref[...] = h
    h_hi = h.astype(BF16)
    h_lo = (h - h_hi.astype(F32)).astype(BF16)
    logits = _dot(h_hi, whi_ref[...]) + _dot(h_lo, whi_ref[...]) + _dot(h_hi, wlo_ref[...])
    lane = lax.broadcasted_iota(jnp.int32, logits.shape, 1)
    logits = jnp.where(lane < N_EXPERTS, logits, -jnp.inf)
    m1 = jnp.max(logits, axis=-1, keepdims=True)
    i1 = jnp.min(jnp.where(logits == m1, lane, LANES), axis=-1, keepdims=True)
    rest = jnp.where(lane == i1, -jnp.inf, logits)
    m2 = jnp.max(rest, axis=-1, keepdims=True)
    i2 = jnp.min(jnp.where(rest == m2, lane, LANES), axis=-1, keepdims=True)
    e = jnp.exp(m2 - m1)
    idx_ref[...] = jnp.concatenate([i1, i2], axis=1)
    gate_ref[...] = jnp.concatenate([1.0 / (1.0 + e), e / (1.0 + e)], axis=1)


def router(x, mod, ng, w_router):
    B, T, D = x.shape
    tm = _tile(T, 512)
    wp = jnp.pad(w_router, ((0, 0), (0, LANES - N_EXPERTS)))
    whi = wp.astype(BF16)
    wlo = (wp - whi.astype(F32)).astype(BF16)
    full = lambda w: pl.BlockSpec(w.shape, lambda bb, i: (0,) * w.ndim)
    row = lambda n: pl.BlockSpec((None, tm, n), lambda bb, i: (bb, i, 0))
    return pl.pallas_call(
        _router_kernel,
        grid=(B, T // tm),
        in_specs=[row(D), pl.BlockSpec((None, 6, D), lambda bb, i: (bb, 0, 0)), full(ng), full(whi), full(wlo)],
        out_specs=[row(D), row(TOP_K), row(TOP_K)],
        out_shape=[jax.ShapeDtypeStruct((B, T, D), F32), jax.ShapeDtypeStruct((B, T, TOP_K), jnp.int32),
                   jax.ShapeDtypeStruct((B, T, TOP_K), F32)],
        compiler_params=_cparams(("parallel", "parallel")),
        name="moe_router",
    )(x, mod, ng, whi, wlo)


def _moe_kernel(blk_e_ref, tok_ref, h_hbm, wg_ref, wu_ref, wd_ref, y_ref, xbuf, xb_ref, acc_ref, sem, *, bm):
    i = pl.program_id(0)
    j = pl.program_id(1)

    def row_copy(r):
        return pltpu.make_async_copy(h_hbm.at[tok_ref[i * bm + r]], xbuf.at[r], sem)

    @pl.when(j == 0)
    def _():
        def start(r, c):
            row_copy(r).start()
            return c

        lax.fori_loop(0, bm, start, 0, unroll=8)
        pltpu.make_async_copy(h_hbm.at[pl.ds(0, bm)], xbuf, sem).wait()
        xb_ref[...] = xbuf[...].astype(BF16)
        acc_ref[...] = jnp.zeros(acc_ref.shape, F32)

    x = xb_ref[...]
    a = (_silu(_dot(x, wg_ref[...])) * _dot(x, wu_ref[...])).astype(BF16)
    acc_ref[...] += _dot(a, wd_ref[...])

    @pl.when(j == pl.num_programs(1) - 1)
    def _():
        y_ref[...] = acc_ref[...]


def moe_experts(h, tok_buf, blk_e, wg, wu, wd, bm):
    N, D = h.shape
    rows = tok_buf.shape[0]
    E, _, F = wg.shape
    tf = F // 7 if F % (7 * LANES) == 0 else F
    return pl.pallas_call(
        functools.partial(_moe_kernel, bm=bm),
        grid_spec=pltpu.PrefetchScalarGridSpec(
            num_scalar_prefetch=2,
            grid=(rows // bm, F // tf),
            in_specs=[pl.BlockSpec(memory_space=pl.ANY),
                      pl.BlockSpec((None, D, tf), lambda i, j, be, tk: (be[i], 0, j)),
                      pl.BlockSpec((None, D, tf), lambda i, j, be, tk: (be[i], 0, j)),
                      pl.BlockSpec((None, tf, D), lambda i, j, be, tk: (be[i], j, 0))],
            out_specs=pl.BlockSpec((bm, D), lambda i, j, be, tk: (i, 0)),
            scratch_shapes=[pltpu.VMEM((bm, D), F32), pltpu.VMEM((bm, D), BF16), pltpu.VMEM((bm, D), F32),
                            pltpu.SemaphoreType.DMA(())],
        ),
        out_shape=jax.ShapeDtypeStruct((rows, D), F32),
        compiler_params=_cparams(("arbitrary", "arbitrary")),
        name="moe_experts",
    )(blk_e, tok_buf, h, wg, wu, wd)


def _moe_combine_kernel(pos_ref, y_hbm, gate_ref, x_ref, mod_ref, ng_ref, o_ref, ybuf, sem, *, tm, rows_per_batch):
    b = pl.program_id(0)
    i = pl.program_id(1)
    base = (b * rows_per_batch + i * tm) * TOP_K

    def start(r, c):
        for kk in range(TOP_K):
            pltpu.make_async_copy(y_hbm.at[pos_ref[base + r * TOP_K + kk]], ybuf.at[kk * tm + r], sem).start()
        return c

    lax.fori_loop(0, tm, start, 0, unroll=4)
    pltpu.make_async_copy(y_hbm.at[pl.ds(0, TOP_K * tm)], ybuf, sem).wait()
    gate = gate_ref[...]
    y = ybuf[0:tm, :] * gate[:, 0:1] + ybuf[tm:2 * tm, :] * gate[:, 1:2]
    o_ref[...] = x_ref[...] + mod_ref[5:6, :] * _rms(y, ng_ref[3:4, :])


def moe_combine(pos, y, gates, x, mod, ng):
    B, T, D = x.shape
    tm = _tile(T, 256)
    return pl.pallas_call(
        functools.partial(_moe_combine_kernel, tm=tm, rows_per_batch=T),
        grid_spec=pltpu.PrefetchScalarGridSpec(
            num_scalar_prefetch=1,
            grid=(B, T // tm),
            in_specs=[pl.BlockSpec(memory_space=pl.ANY),
                      pl.BlockSpec((None, tm, TOP_K), lambda b, i, p: (b, i, 0)),
                      pl.BlockSpec((None, tm, D), lambda b, i, p: (b, i, 0)),
                      pl.BlockSpec((None, 6, D), lambda b, i, p: (b, 0, 0)),
                      pl.BlockSpec(ng.shape, lambda b, i, p: (0, 0))],
            out_specs=pl.BlockSpec((None, tm, D), lambda b, i, p: (b, i, 0)),
            scratch_shapes=[pltpu.VMEM((TOP_K * tm, D), F32), pltpu.SemaphoreType.DMA(())],
        ),
        out_shape=jax.ShapeDtypeStruct((B, T, D), F32),
        compiler_params=_cparams(("arbitrary", "arbitrary")),
        name="moe_combine",
    )(pos, y, gates, x, mod, ng)


def _routing_tables(top_idx, bm):
    n = top_idx.shape[0]
    flat_e = top_idx.reshape(-1)
    onehot = (flat_e[:, None] == jnp.arange(N_EXPERTS, dtype=jnp.int32)[None, :]).astype(jnp.int32)
    csum = jnp.cumsum(onehot, axis=0)
    rank = jnp.sum(csum * onehot, axis=1) - 1
    counts = csum[-1]
    padded = (counts + bm - 1) // bm * bm
    pad_end = jnp.cumsum(padded)
    pad_start = pad_end - padded
    pos = jnp.sum(pad_start[None, :] * onehot, axis=1) + rank
    rows = n * TOP_K + N_EXPERTS * bm
    flat_tok = jnp.arange(n * TOP_K, dtype=jnp.int32) // TOP_K
    tok_buf = jnp.zeros((rows,), jnp.int32).at[pos].set(flat_tok)
    blk_start = jnp.arange(rows // bm, dtype=jnp.int32) * bm
    blk_e = jnp.minimum(jnp.searchsorted(pad_end, blk_start, side='right'), N_EXPERTS - 1).astype(jnp.int32)
    return pos.astype(jnp.int32), tok_buf, blk_e


def moe(x, mod, ng, w_router, wg, wu, wd, bm):
    B, T, D = x.shape
    h, top_idx, gates = router(x, mod, ng, w_router)
    pos, tok_buf, blk_e = _routing_tables(top_idx.reshape(B * T, TOP_K), bm)
    y = moe_experts(h.reshape(B * T, D), tok_buf, blk_e, wg, wu, wd, bm)
    return moe_combine(pos, y, gates, x, mod, ng)


def _prep_weights(p):
    w = {}
    w['l0'] = _prep_l0_weights(p['w_in_ab'][0], p['mla_w_qb'][0], p['mla_w_kvb'][0])
    w['qn'] = p['mla_q_norm'][0][None, :]
    w['kvn'] = p['mla_kv_norm'][0][None, :]
    w_out = p['w_out_ab'][0]
    na = MLA_HEADS * MLA_V
    g = SWA_HEADS // SWA_KV_HEADS
    D = w_out.shape[1]
    wa = jnp.pad(w_out[:na].reshape(MLA_HEADS, MLA_V, D), ((0, 0), (0, LANES - MLA_V), (0, 0)))
    w['wa'] = wa.reshape(MLA_HEADS * LANES, D).astype(BF16)
    w['wb'] = w_out[na:].reshape(SWA_KV_HEADS, g, SWA_HEAD_DIM, D).transpose(1, 0, 2, 3).reshape(-1, D).astype(BF16)
    w['ffn'] = tuple(p[k][0].astype(BF16) for k in ('ffn_w_gate', 'ffn_w_up', 'ffn_w_down'))
    w_in_c = p['w_in_c'][0]
    ncol = w_in_c.shape[1]
    w['win_c'] = jnp.pad(w_in_c, ((0, 0), (0, LANES - 2 * GLA_GATE_RANK))).astype(BF16)
    wgu = p['gla_w_gate_up'][0]
    nk = wgu.shape[-1]
    zero = jnp.zeros((GLA_GATE_RANK, nk), F32)
    wgate = jnp.concatenate([jnp.concatenate([wgu[0], zero], axis=1), jnp.concatenate([zero, wgu[1]], axis=1)], axis=0)
    w['wgate'] = jnp.pad(wgate, ((0, LANES - 2 * GLA_GATE_RANK), (0, 0))).astype(BF16)
    w['bgate'] = p['gla_b_gate_up'][0].reshape(1, 2 * nk)
    w['nk'] = nk
    w['nv'] = (ncol - 2 * nk - 2 * GLA_GATE_RANK) // 2
    w['out_norm'] = p['gla_out_norm'][0][None, :]
    w['w_out_c'] = p['w_out_c'][0].astype(BF16)
    w['router'] = p['moe_router'][0]
    w['moe'] = tuple(p[k][0].astype(BF16) for k in ('moe_w_gate', 'moe_w_up', 'moe_w_down'))
    w['swa_bias'] = swa_bias(p['rel_bias'])
    w['sink'] = p['swa_sink'][0]
    return w


def _trunk(x, c, p, w, moe_bm):
    B, T, D = x.shape
    tables = _rope_tables(T)
    mod = ada_mod(c, p['w_ada'], p['b_ada'], 0)
    ng = p['norm_gain'][0]
    q, k, v, sq, sk, sv = l0_proj(x, mod, ng, w['l0'][0], w['qn'], w['kvn'], *w['l0'][1:], tables)
    a = mla_attention(q, k, v)
    b = swa_attention(sq, sk, sv, w['sink'], w['swa_bias'])
    x = out_res(a, b, w['wa'], w['wb'], x, mod, ng)
    x = ffn(x, mod, ng, *w['ffn'])
    mod = ada_mod(c, p['w_ada'], p['b_ada'], 1)
    ng = p['norm_gain'][1]
    q, k, v, r, gf, gb = l1_proj(x, mod, ng, w['win_c'], w['wgate'], w['bgate'], w['nk'], w['nv'])
    o_f = gla_scan(q, k, gf, v, reverse=False)
    o_b = gla_scan(q, k, gb, v, reverse=True)
    x = gla_out(o_f, o_b, r, w['out_norm'], w['w_out_c'], x, mod, ng)
    return moe(x, mod, ng, w['router'], *w['moe'], moe_bm)


def kernel(x_prompt, x_sample, c_prompt, c_sample, norm_gain, w_ada, b_ada, rel_bias, w_in_ab, mla_q_norm, mla_kv_norm, mla_w_qb, mla_w_kvb, swa_sink, w_out_ab, ffn_w_gate, ffn_w_up, ffn_w_down, w_in_c, gla_w_gate_up, gla_b_gate_up, gla_out_norm, w_out_c, moe_router, moe_w_gate, moe_w_up, moe_w_down):
    p = dict(norm_gain=norm_gain, w_ada=w_ada, b_ada=b_ada, rel_bias=rel_bias, w_in_ab=w_in_ab,
             mla_q_norm=mla_q_norm, mla_kv_norm=mla_kv_norm, mla_w_qb=mla_w_qb, mla_w_kvb=mla_w_kvb,
             swa_sink=swa_sink, w_out_ab=w_out_ab, ffn_w_gate=ffn_w_gate, ffn_w_up=ffn_w_up,
             ffn_w_down=ffn_w_down, w_in_c=w_in_c, gla_w_gate_up=gla_w_gate_up, gla_b_gate_up=gla_b_gate_up,
             gla_out_norm=gla_out_norm, w_out_c=w_out_c, moe_router=moe_router, moe_w_gate=moe_w_gate,
             moe_w_up=moe_w_up, moe_w_down=moe_w_down)
    w = _prep_weights(p)
    bm = 512
    return (_trunk(x_prompt, c_prompt, p, w, bm), _trunk(x_sample, c_sample, p, w, bm))
```

```python
import functools
import math

import jax
import jax.numpy as jnp
from jax import lax
from jax.experimental import pallas as pl
from jax.experimental.pallas import tpu as pltpu

F32 = jnp.float32
BF16 = jnp.bfloat16

MLA_HEADS = 8
MLA_NOPE = 64
MLA_ROPE = 32
MLA_V = 64
MLA_Q_LORA = 384
MLA_KV_LORA = 256
ROPE_THETA = 10000.0
SWA_HEADS = 8
SWA_KV_HEADS = 2
SWA_HEAD_DIM = 64
WINDOW = 128
REL_BUCKETS = 32
REL_MAX_DIST = 128
GLA_HEADS = 4
GLA_GATE_RANK = 16
GLA_GATE_NORM = 16.0
GLA_CHUNK = 64
N_EXPERTS = 8
TOP_K = 2
EPS = 1e-6

LANES = 128
VMEM_LIMIT = 56 * 1024 * 1024


def _cparams(sem, vmem=VMEM_LIMIT):
    return pltpu.CompilerParams(dimension_semantics=sem, vmem_limit_bytes=vmem)


def _tile(n, pref):
    if n <= pref:
        return n
    t = pref
    while n % t:
        t -= 8
    return t


def _rms(x, g):
    return x * lax.rsqrt(jnp.mean(x * x, axis=-1, keepdims=True) + EPS) * g


def _modulate(x, g, shift, scale):
    return _rms(x, g) * (1.0 + scale) + shift


def _silu(x):
    return x / (1.0 + jnp.exp(-x))


def _dot(a, b):
    return jnp.dot(a, b, preferred_element_type=F32)


def _dot_nt(a, b):
    return lax.dot_general(a, b, (((1,), (1,)), ((), ())), preferred_element_type=F32)


def _dot_tn(a, b):
    return lax.dot_general(a, b, (((0,), (0,)), ((), ())), preferred_element_type=F32)


def _ada_kernel(c_ref, w_ref, b_ref, o_ref):
    s = _silu(c_ref[...]).astype(BF16)
    o_ref[...] = _dot(s, w_ref[...].astype(BF16)) + b_ref[...]


def ada_mod(c, w_ada, b_ada, layer):
    B, D = c.shape
    N = w_ada.shape[-1]
    Bp = -(-B // 16) * 16
    cp = jnp.pad(c, ((0, Bp - B), (0, 0)))
    tn = _tile(N, 1536)
    out = pl.pallas_call(
        _ada_kernel,
        grid=(N // tn,),
        in_specs=[pl.BlockSpec((Bp, D), lambda j: (0, 0)),
                  pl.BlockSpec((None, D, tn), lambda j: (layer, 0, j)),
                  pl.BlockSpec((None, 1, tn), lambda j: (layer, 0, j))],
        out_specs=pl.BlockSpec((Bp, tn), lambda j: (0, j)),
        out_shape=jax.ShapeDtypeStruct((Bp, N), F32),
        compiler_params=_cparams(("parallel",)),
        name="ada_mod",
    )(cp, w_ada, b_ada.reshape(b_ada.shape[0], 1, N))
    return out[:B].reshape(B, 6, D)


def _l0_proj_kernel(x_ref, mod_ref, ng_ref, win_ref, qn_ref, kvn_ref, wq_ref, wqs_ref, wk_ref, wv_ref,
                    cq_ref, sq_ref, ck_ref, sk_ref,
                    q_ref, k_ref, v_ref, sq_out_ref, sk_out_ref, sv_out_ref):
    h = _modulate(x_ref[...], ng_ref[0:1, :], mod_ref[0:1, :], mod_ref[1:2, :]).astype(BF16)
    p = _dot(h, win_ref[...])
    o1 = MLA_Q_LORA
    o2 = o1 + MLA_KV_LORA
    nq = _rms(p[:, :o1], qn_ref[...]).astype(BF16)
    q = _dot(nq, wq_ref[...])
    qs = _dot(nq, wqs_ref[...])
    cq = jnp.concatenate([cq_ref[...]] * MLA_HEADS, axis=1)
    sq = jnp.concatenate([sq_ref[...]] * MLA_HEADS, axis=1)
    q_ref[...] = (q * cq + qs * sq).astype(q_ref.dtype)
    nkv = _rms(p[:, o1:o2], kvn_ref[...]).astype(BF16)
    kpe = p[:, o2:o2 + LANES] * ck_ref[...] + p[:, o2 + LANES:o2 + 2 * LANES] * sk_ref[...]
    k = _dot(nkv, wk_ref[...]) + jnp.concatenate([kpe] * MLA_HEADS, axis=1)
    k_ref[...] = k.astype(k_ref.dtype)
    v = _dot(nkv, wv_ref[...])
    lane = lax.broadcasted_iota(jnp.int32, v.shape, 1)
    v_ref[...] = jnp.where((lane & (LANES - 1)) == MLA_V, 1.0, v).astype(v_ref.dtype)
    o3 = o2 + 2 * LANES
    o4 = o3 + SWA_HEADS * LANES
    sq_out_ref[...] = (p[:, o3:o4] * (SWA_HEAD_DIM ** -0.5)).astype(sq_out_ref.dtype)
    sk_out_ref[...] = p[:, o4:o4 + LANES].astype(sk_out_ref.dtype)
    sv_out_ref[...] = p[:, o4 + LANES:o4 + 2 * LANES].astype(sv_out_ref.dtype)


def _prep_l0_weights(w_in, w_qb, w_kvb):
    D = w_in.shape[0]
    o1 = MLA_Q_LORA
    o2 = o1 + MLA_KV_LORA
    o2r = o2 + MLA_ROPE
    half = MLA_ROPE // 2
    padl = lambda w, l, r: jnp.pad(w, ((0, 0), (l, r)))
    kpe = w_in[:, o2:o2r]
    kpe_sw = jnp.concatenate([-kpe[:, half:], kpe[:, :half]], axis=1)
    o3 = o2r + SWA_HEADS * SWA_HEAD_DIM
    o4 = o3 + SWA_KV_HEADS * SWA_HEAD_DIM
    g = SWA_HEADS // SWA_KV_HEADS
    swa_q = []
    for h in range(SWA_HEADS):
        hk = h // g
        blk = w_in[:, o2r + h * SWA_HEAD_DIM: o2r + (h + 1) * SWA_HEAD_DIM]
        swa_q.append(padl(blk, hk * SWA_HEAD_DIM, LANES - (hk + 1) * SWA_HEAD_DIM))
    win = jnp.concatenate(
        [w_in[:, :o2], padl(kpe, MLA_NOPE, LANES - MLA_NOPE - MLA_ROPE),
         padl(kpe_sw, MLA_NOPE, LANES - MLA_NOPE - MLA_ROPE)] + swa_q + [w_in[:, o3:o4], w_in[:, o4:]],
        axis=1).astype(BF16)
    dqk = MLA_NOPE + MLA_ROPE
    wq3 = w_qb.reshape(o1, MLA_HEADS, dqk)
    wq = jnp.pad(wq3, ((0, 0), (0, 0), (0, LANES - dqk))).reshape(o1, MLA_HEADS * LANES).astype(BF16)
    x1 = wq3[..., MLA_NOPE:MLA_NOPE + half]
    x2 = wq3[..., MLA_NOPE + half:]
    z = jnp.zeros_like
    wqs = jnp.concatenate([z(wq3[..., :MLA_NOPE]), -x2, x1, z(wq3[..., :LANES - dqk])], axis=-1)
    wqs = wqs.reshape(o1, MLA_HEADS * LANES).astype(BF16)
    wkv3 = w_kvb.reshape(MLA_KV_LORA, MLA_HEADS, MLA_NOPE + MLA_V)
    wk = jnp.pad(wkv3[..., :MLA_NOPE], ((0, 0), (0, 0), (0, LANES - MLA_NOPE)))
    wk = wk.reshape(MLA_KV_LORA, MLA_HEADS * LANES).astype(BF16)
    wv = jnp.pad(wkv3[..., MLA_NOPE:], ((0, 0), (0, 0), (0, LANES - MLA_V)))
    wv = wv.reshape(MLA_KV_LORA, MLA_HEADS * LANES).astype(BF16)
    return win, wq, wqs, wk, wv


def _rope_tables(T):
    half = MLA_ROPE // 2
    inv = 1.0 / (ROPE_THETA ** (jnp.arange(half, dtype=F32) / half))
    ang = jnp.arange(T, dtype=F32)[:, None] * inv[None, :]
    cos, sin = jnp.cos(ang), jnp.sin(ang)
    s = (MLA_NOPE + MLA_ROPE) ** -0.5 * math.log2(math.e)
    one = jnp.ones((T, MLA_NOPE), F32)
    zl = jnp.zeros((T, MLA_NOPE), F32)
    zr = jnp.zeros((T, LANES - MLA_NOPE - MLA_ROPE), F32)
    cq = jnp.concatenate([one * s, cos * s, cos * s, zr], axis=1)
    sq = jnp.concatenate([zl, sin * s, sin * s, zr], axis=1)
    ck = jnp.concatenate([zl, cos, cos, zr], axis=1)
    sk = jnp.concatenate([zl, sin, sin, zr], axis=1)
    return cq, sq, ck, sk


def l0_proj(x, mod, ng, win, qn, kvn, wq, wqs, wk, wv, tables):
    B, T, D = x.shape
    tm = _tile(T, 512)
    NP = win.shape[1]
    HQ = MLA_HEADS * LANES
    full = lambda a: pl.BlockSpec(a.shape, lambda b, i: (0,) * a.ndim)
    row = lambda n: pl.BlockSpec((None, tm, n), lambda b, i: (b, i, 0))
    tab = pl.BlockSpec((tm, LANES), lambda b, i: (i, 0))
    outs = [(HQ, BF16), (HQ, BF16), (HQ, BF16), (SWA_HEADS * LANES, BF16), (LANES, BF16), (LANES, BF16)]
    return pl.pallas_call(
        _l0_proj_kernel,
        grid=(B, T // tm),
        in_specs=[row(D), pl.BlockSpec((None, 6, D), lambda b, i: (b, 0, 0)), full(ng), full(win), full(qn),
                  full(kvn), full(wq), full(wqs), full(wk), full(wv), tab, tab, tab, tab],
        out_specs=[row(n) for n, _ in outs],
        out_shape=[jax.ShapeDtypeStruct((B, T, n), dt) for n, dt in outs],
        compiler_params=_cparams(("parallel", "parallel")),
        name="l0_proj",
    )(x, mod, ng, win, qn, kvn, wq, wqs, wk, wv, *tables)


def _mla_attn_kernel(q_ref, k_ref, v_ref, o_ref, m_ref, acc_ref, *, tk):
    T = k_ref.shape[0]
    nk = T // tk
    m_ref[...] = jnp.full(m_ref.shape, -jnp.inf, F32)
    acc_ref[...] = jnp.zeros(acc_ref.shape, F32)
    q = q_ref[...]
    per_trip = next(u for u in (4, 2, 1) if nk % u == 0)

    def body(j, carry):
        r0 = [pl.multiple_of((j * per_trip + u) * tk, tk) for u in range(per_trip)]
        s = [_dot_nt(q, k_ref[pl.ds(r, tk), :]) for r in r0]
        m = m_ref[...]
        acc = acc_ref[...]
        for u in range(per_trip):
            m_new = jnp.maximum(m, jnp.max(s[u], axis=-1, keepdims=True))
            p = jnp.exp2(s[u] - jnp.tile(m_new, (1, tk // LANES)))
            acc = jnp.exp2(m - m_new) * acc + _dot(p.astype(BF16), v_ref[pl.ds(r0[u], tk), :])
            m = m_new
        m_ref[...] = m
        acc_ref[...] = acc
        return carry

    lax.fori_loop(0, nk // per_trip, body, 0)
    acc = acc_ref[...]
    o_ref[...] = (acc / acc[:, MLA_V:MLA_V + 1]).astype(o_ref.dtype)


def mla_attention(q, k, v):
    B, T, _ = q.shape
    tq = _tile(T, 512)
    tk = _tile(T, 1024)
    qspec = pl.BlockSpec((None, tq, LANES), lambda b, h, i: (b, i, h))
    kspec = pl.BlockSpec((None, T, LANES), lambda b, h, i: (b, 0, h))
    return pl.pallas_call(
        functools.partial(_mla_attn_kernel, tk=tk),
        grid=(B, MLA_HEADS, T // tq),
        in_specs=[qspec, kspec, kspec],
        out_specs=qspec,
        out_shape=jax.ShapeDtypeStruct((B, T, MLA_HEADS * LANES), BF16),
        scratch_shapes=[pltpu.VMEM((tq, LANES), F32), pltpu.VMEM((tq, LANES), F32)],
        compiler_params=_cparams(("parallel", "parallel", "parallel")),
        name="mla_attention",
    )(q, k, v)


def _t5_bucket(rel):
    nb = REL_BUCKETS // 2
    max_exact = nb // 2
    n = jnp.abs(rel)
    big = max_exact + (jnp.log(jnp.maximum(n, 1).astype(F32) / max_exact)
                       / math.log(REL_MAX_DIST / max_exact) * (nb - max_exact)).astype(jnp.int32)
    big = jnp.minimum(big, nb - 1)
    return jnp.where(rel > 0, nb, 0) + jnp.where(n < max_exact, n, big)


def _swa_bias_kernel(rb_ref, bucket_ref, o_ref):
    h = pl.program_id(0)
    bucket = bucket_ref[...]
    acc = jnp.zeros(bucket.shape, F32)
    for b in range(REL_BUCKETS):
        acc = jnp.where(bucket == b, rb_ref[b * SWA_HEADS + h], acc)
    qi = lax.broadcasted_iota(jnp.int32, bucket.shape, 0)
    kj = lax.broadcasted_iota(jnp.int32, bucket.shape, 1)
    rel = kj - WINDOW - qi
    o_ref[...] = jnp.where(jnp.abs(rel) <= WINDOW, acc, -jnp.inf)


def swa_bias(rel_bias):
    W = WINDOW
    qi = jnp.arange(W, dtype=jnp.int32)[:, None]
    kj = jnp.arange(3 * W, dtype=jnp.int32)[None, :]
    bucket = _t5_bucket(kj - W - qi)
    return pl.pallas_call(
        _swa_bias_kernel,
        grid=(SWA_HEADS,),
        in_specs=[pl.BlockSpec(memory_space=pltpu.SMEM), pl.BlockSpec((W, 3 * W), lambda h: (0, 0))],
        out_specs=pl.BlockSpec((None, W, 3 * W), lambda h: (h, 0, 0)),
        out_shape=jax.ShapeDtypeStruct((SWA_HEADS, W, 3 * W), F32),
        name="swa_bias",
    )(rel_bias.reshape(-1), bucket)


def _swa_kernel(sink_ref, q_ref, kp_ref, kc_ref, kn_ref, vp_ref, vc_ref, vn_ref, bias_ref, o_ref):
    n = pl.program_id(1)
    nb = pl.num_programs(1)
    W = WINDOW
    kband = jnp.concatenate([kp_ref[...], kc_ref[...], kn_ref[...]], axis=0)
    vband = jnp.concatenate([vp_ref[...], vc_ref[...], vn_ref[...]], axis=0)
    col = lax.broadcasted_iota(jnp.int32, (W, 3 * W), 1)
    lo = jnp.where(n == 0, W, 0)
    hi = jnp.where(n == nb - 1, 2 * W, 3 * W)
    inside = (col >= lo) & (col < hi)
    lane = lax.broadcasted_iota(jnp.int32, (W, LANES), 1)
    G = SWA_HEADS // SWA_KV_HEADS
    heads = range(SWA_HEADS)
    s = [_dot_nt(q_ref[:, h * LANES:(h + 1) * LANES], kband) for h in heads]
    s = [jnp.where(inside, s[h] + bias_ref[h], -jnp.inf) for h in heads]
    m = [jnp.maximum(jnp.max(s[h], axis=-1, keepdims=True), sink_ref[h]) for h in heads]
    p = [jnp.exp(s[h] - m[h]) for h in heads]
    den = [jnp.sum(p[h], axis=-1, keepdims=True) + jnp.exp(sink_ref[h] - m[h]) for h in heads]
    o = [_dot(p[h].astype(BF16), vband) for h in heads]
    o = [o[h] / den[h] for h in heads]
    outs = [jnp.where(lane < SWA_HEAD_DIM, o[g], o[G + g]) for g in range(G)]
    o_ref[...] = jnp.concatenate(outs, axis=1).astype(o_ref.dtype)


def swa_attention(q, k, v, sink, bias):
    B, T, _ = q.shape
    W = WINDOW
    nb = T // W
    G = SWA_HEADS // SWA_KV_HEADS
    prev = pl.BlockSpec((None, W, LANES), lambda b, n: (b, jnp.maximum(n - 1, 0), 0))
    cur = pl.BlockSpec((None, W, LANES), lambda b, n: (b, n, 0))
    nxt = pl.BlockSpec((None, W, LANES), lambda b, n: (b, jnp.minimum(n + 1, nb - 1), 0))
    return pl.pallas_call(
        _swa_kernel,
        grid=(B, nb),
        in_specs=[pl.BlockSpec(memory_space=pltpu.SMEM),
                  pl.BlockSpec((None, W, SWA_HEADS * LANES), lambda b, n: (b, n, 0)),
                  prev, cur, nxt, prev, cur, nxt,
                  pl.BlockSpec(bias.shape, lambda b, n: (0, 0, 0))],
        out_specs=pl.BlockSpec((None, W, G * LANES), lambda b, n: (b, n, 0)),
        out_shape=jax.ShapeDtypeStruct((B, T, G * LANES), BF16),
        compiler_params=_cparams(("parallel", "parallel")),
        name="swa_attention",
    )(sink, q, k, k, k, v, v, v, bias)


def _out_res_kernel(a_ref, b_ref, wa_ref, wb_ref, x_ref, mod_ref, ng_ref, o_ref, *, gate_row, gain_row):
    y = _dot(a_ref[...], wa_ref[...]) + _dot(b_ref[...], wb_ref[...])
    o_ref[...] = x_ref[...] + mod_ref[gate_row:gate_row + 1, :] * _rms(y, ng_ref[gain_row:gain_row + 1, :])


def out_res(a, b, wa, wb, x, mod, ng):
    B, T, D = x.shape
    tm = _tile(T, 512)
    full = lambda w: pl.BlockSpec(w.shape, lambda bb, i: (0,) * w.ndim)
    row = lambda n: pl.BlockSpec((None, tm, n), lambda bb, i: (bb, i, 0))
    return pl.pallas_call(
        functools.partial(_out_res_kernel, gate_row=2, gain_row=1),
        grid=(B, T // tm),
        in_specs=[row(a.shape[-1]), row(b.shape[-1]), full(wa), full(wb), row(D),
                  pl.BlockSpec((None, 6, D), lambda bb, i: (bb, 0, 0)), full(ng)],
        out_specs=row(D),
        out_shape=jax.ShapeDtypeStruct((B, T, D), F32),
        compiler_params=_cparams(("parallel", "parallel")),
        name="out_res",
    )(a, b, wa, wb, x, mod, ng)


def _ffn_kernel(x_ref, mod_ref, ng_ref, wg_ref, wu_ref, wd_ref, o_ref):
    x = x_ref[...]
    h = _modulate(x, ng_ref[2:3, :], mod_ref[3:4, :], mod_ref[4:5, :]).astype(BF16)
    a = (_silu(_dot(h, wg_ref[...])) * _dot(h, wu_ref[...])).astype(BF16)
    o_ref[...] = x + mod_ref[5:6, :] * _rms(_dot(a, wd_ref[...]), ng_ref[3:4, :])


def ffn(x, mod, ng, wg, wu, wd):
    B, T, D = x.shape
    tm = _tile(T, 512)
    row = pl.BlockSpec((None, tm, D), lambda b, i: (b, i, 0))
    resident = lambda w: pl.BlockSpec(w.shape, lambda b, i: (0, 0), pipeline_mode=pl.Buffered(1))
    return pl.pallas_call(
        _ffn_kernel,
        grid=(B, T // tm),
        in_specs=[row, pl.BlockSpec((None, 6, D), lambda b, i: (b, 0, 0)),
                  pl.BlockSpec(ng.shape, lambda b, i: (0, 0)), resident(wg), resident(wu), resident(wd)],
        out_specs=row,
        out_shape=jax.ShapeDtypeStruct((B, T, D), F32),
        compiler_params=_cparams(("parallel", "parallel")),
        name="ffn",
    )(x, mod, ng, wg, wu, wd)


def _l1_proj_kernel(x_ref, mod_ref, ng_ref, win_ref, wgate_ref, bgate_ref,
                    q_ref, k_ref, v_ref, r_ref, gf_ref, gb_ref):
    h = _modulate(x_ref[...], ng_ref[0:1, :], mod_ref[0:1, :], mod_ref[1:2, :]).astype(BF16)
    p = _dot(h, win_ref[...])
    nk = q_ref.shape[-1]
    nv = v_ref.shape[-1]
    dk = nk // GLA_HEADS
    q_ref[...] = p[:, :nk] * (dk ** -0.5)
    k_ref[...] = p[:, nk:2 * nk]
    v_ref[...] = p[:, 2 * nk:2 * nk + nv].astype(v_ref.dtype)
    r_ref[...] = p[:, 2 * nk + nv:2 * nk + 2 * nv].astype(r_ref.dtype)
    gd = p[:, 2 * nk + 2 * nv:].astype(BF16)
    pre = _dot(gd, wgate_ref[...]) + bgate_ref[...]
    logsig = jnp.minimum(pre, 0.0) - jnp.log1p(jnp.exp(-jnp.abs(pre)))
    g = logsig / GLA_GATE_NORM
    gf_ref[...] = g[:, :nk]
    gb_ref[...] = g[:, nk:]


def l1_proj(x, mod, ng, win, wgate, bgate, nk, nv):
    B, T, D = x.shape
    tm = _tile(T, 512)
    full = lambda a: pl.BlockSpec(a.shape, lambda b, i: (0,) * a.ndim)
    row = lambda n: pl.BlockSpec((None, tm, n), lambda b, i: (b, i, 0))
    outs = [(nk, F32), (nk, F32), (nv, BF16), (nv, BF16), (nk, F32), (nk, F32)]
    return pl.pallas_call(
        _l1_proj_kernel,
        grid=(B, T // tm),
        in_specs=[row(D), pl.BlockSpec((None, 6, D), lambda b, i: (b, 0, 0)), full(ng), full(win), full(wgate),
                  full(bgate)],
        out_specs=[row(n) for n, _ in outs],
        out_shape=[jax.ShapeDtypeStruct((B, T, n), dt) for n, dt in outs],
        compiler_params=_cparams(("parallel", "parallel")),
        name="l1_proj",
    )(x, mod, ng, win, wgate, bgate)


def _gla_kernel(q_ref, k_ref, g_ref, v_ref, o_ref, st_ref, *, reverse):
    t = pl.program_id(2)
    L = GLA_CHUNK
    nchunk = q_ref.shape[0] // L

    @pl.when(t == 0)
    def _():
        st_ref[...] = jnp.zeros(st_ref.shape, F32)

    ri = lax.broadcasted_iota(jnp.int32, (L, L), 0)
    ci = lax.broadcasted_iota(jnp.int32, (L, L), 1)
    keep = (ci >= ri) if reverse else (ci <= ri)
    order = list(range(nchunk - 1, -1, -1) if reverse else range(nchunk))
    chunks = [slice(c * L, (c + 1) * L) for c in range(nchunk)]

    b = g_ref[...]
    n = b.shape[0]
    pos = lax.broadcasted_iota(jnp.int32, b.shape, 0) & (L - 1)
    step = 1
    while step < L:
        if reverse:
            b = b + jnp.where(pos < L - step, pltpu.roll(b, n - step, axis=0), 0.0)
        else:
            b = b + jnp.where(pos >= step, pltpu.roll(b, step, axis=0), 0.0)
        step *= 2
    ends = [b[c * L:c * L + 1, :] if reverse else b[(c + 1) * L - 1:(c + 1) * L, :] for c in range(nchunk)]
    b_end = jnp.concatenate([jnp.broadcast_to(e, (L, e.shape[1])) for e in ends], axis=0)
    k = k_ref[...]
    qe = (q_ref[...] * jnp.exp(b)).astype(BF16)
    ke = (k * jnp.exp(-b)).astype(BF16)
    kd = (k * jnp.exp(b_end - b)).astype(BF16)
    v = v_ref[...]
    a = [_dot_nt(qe[r], ke[r]) for r in chunks]
    a = [jnp.where(keep, x, 0.0).astype(BF16) for x in a]
    intra = [_dot(a[c], v[chunks[c]]) for c in range(nchunk)]
    upd = [_dot_tn(v[r], kd[r]) for r in chunks]
    st = st_ref[...]
    outs = [None] * nchunk
    for c in order:
        outs[c] = (intra[c] + _dot_nt(qe[chunks[c]], st.astype(BF16))).astype(o_ref.dtype)
        st = st * jnp.exp(ends[c]) + upd[c]
    st_ref[...] = st
    o_ref[...] = jnp.concatenate(outs, axis=0)


def gla_scan(q, k, g, v, reverse):
    B, T, nk = q.shape
    nv = v.shape[-1]
    dk, dv = nk // GLA_HEADS, nv // GLA_HEADS
    tc = _tile(T, 512)
    nt = T // tc
    blk = (lambda t: nt - 1 - t) if reverse else (lambda t: t)
    qspec = pl.BlockSpec((None, tc, dk), lambda b, h, t: (b, blk(t), h))
    vspec = pl.BlockSpec((None, tc, dv), lambda b, h, t: (b, blk(t), h))
    return pl.pallas_call(
        functools.partial(_gla_kernel, reverse=reverse),
        grid=(B, GLA_HEADS, nt),
        in_specs=[qspec, qspec, qspec, vspec],
        out_specs=vspec,
        out_shape=jax.ShapeDtypeStruct((B, T, nv), F32),
        scratch_shapes=[pltpu.VMEM((dv, dk), F32)],
        compiler_params=_cparams(("parallel", "parallel", "arbitrary")),
        name="gla_bwd" if reverse else "gla_fwd",
    )(q, k, g, v)


def _gla_out_kernel(of_ref, ob_ref, r_ref, on_ref, w_ref, x_ref, mod_ref, ng_ref, o_ref):
    o = of_ref[...] + ob_ref[...]
    dv = on_ref.shape[-1]
    parts = [_rms(o[:, h * dv:(h + 1) * dv], on_ref[...]) for h in range(GLA_HEADS)]
    on = jnp.concatenate(parts, axis=1)
    y = _dot((on * _silu(r_ref[...].astype(F32))).astype(BF16), w_ref[...])
    o_ref[...] = x_ref[...] + mod_ref[2:3, :] * _rms(y, ng_ref[1:2, :])


def gla_out(o_f, o_b, r, out_norm, w_out, x, mod, ng):
    B, T, D = x.shape
    tm = _tile(T, 512)
    nv = o_f.shape[-1]
    full = lambda w: pl.BlockSpec(w.shape, lambda bb, i: (0,) * w.ndim)
    row = lambda n: pl.BlockSpec((None, tm, n), lambda bb, i: (bb, i, 0))
    return pl.pallas_call(
        _gla_out_kernel,
        grid=(B, T // tm),
        in_specs=[row(nv), row(nv), row(nv), full(out_norm), full(w_out), row(D),
                  pl.BlockSpec((None, 6, D), lambda bb, i: (bb, 0, 0)), full(ng)],
        out_specs=row(D),
        out_shape=jax.ShapeDtypeStruct((B, T, D), F32),
        compiler_params=_cparams(("parallel", "parallel")),
        name="gla_out",
    )(o_f, o_b, r, out_norm, w_out, x, mod, ng)


def _router_kernel(x_ref, mod_ref, ng_ref, whi_ref, wlo_ref, h_ref, idx_ref, gate_ref):
    h = _modulate(x_ref[...], ng_ref[2:3, :], mod_ref[3:4, :], mod_ref[4:5, :])
    h_ref[...] = h
    h_hi = h.astype(BF16)
    h_lo = (h - h_hi.astype(F32)).astype(BF16)
    logits = _dot(h_hi, whi_ref[...]) + _dot(h_lo, whi_ref[...]) + _dot(h_hi, wlo_ref[...])
    lane = lax.broadcasted_iota(jnp.int32, logits.shape, 1)
    logits = jnp.where(lane < N_EXPERTS, logits, -jnp.inf)
    m1 = jnp.max(logits, axis=-1, keepdims=True)
    i1 = jnp.min(jnp.where(logits == m1, lane, LANES), axis=-1, keepdims=True)
    rest = jnp.where(lane == i1, -jnp.inf, logits)
    m2 = jnp.max(rest, axis=-1, keepdims=True)
    i2 = jnp.min(jnp.where(rest == m2, lane, LANES), axis=-1, keepdims=True)
    e = jnp.exp(m2 - m1)
    idx_ref[...] = jnp.concatenate([i1, i2], axis=1)
    gate_ref[...] = jnp.concatenate([1.0 / (1.0 + e), e / (1.0 + e)], axis=1)


def router(x, mod, ng, w_router):
    B, T, D = x.shape
    tm = _tile(T, 512)
    wp = jnp.pad(w_router, ((0, 0), (0, LANES - N_EXPERTS)))
    whi = wp.astype(BF16)
    wlo = (wp - whi.astype(F32)).astype(BF16)
    full = lambda w: pl.BlockSpec(w.shape, lambda bb, i: (0,) * w.ndim)
    row = lambda n: pl.BlockSpec((None, tm, n), lambda bb, i: (bb, i, 0))
    return pl.pallas_call(
        _router_kernel,
        grid=(B, T // tm),
        in_specs=[row(D), pl.BlockSpec((None, 6, D), lambda bb, i: (bb, 0, 0)), full(ng), full(whi), full(wlo)],
        out_specs=[row(D), row(TOP_K), row(TOP_K)],
        out_shape=[jax.ShapeDtypeStruct((B, T, D), F32), jax.ShapeDtypeStruct((B, T, TOP_K), jnp.int32),
                   jax.ShapeDtypeStruct((B, T, TOP_K), F32)],
        compiler_params=_cparams(("parallel", "parallel")),
        name="moe_router",
    )(x, mod, ng, whi, wlo)


def _moe_kernel(blk_e_ref, tok_ref, h_hbm, wg_ref, wu_ref, wd_ref, y_ref, xbuf, xb_ref, acc_ref, sem, *, bm):
    i = pl.program_id(0)
    j = pl.program_id(1)

    def row_copy(r):
        return pltpu.make_async_copy(h_hbm.at[tok_ref[i * bm + r]], xbuf.at[r], sem)

    @pl.when(j == 0)
    def _():
        def start(r, c):
            row_copy(r).start()
            return c

        lax.fori_loop(0, bm, start, 0, unroll=8)
        pltpu.make_async_copy(h_hbm.at[pl.ds(0, bm)], xbuf, sem).wait()
        xb_ref[...] = xbuf[...].astype(BF16)
        acc_ref[...] = jnp.zeros(acc_ref.shape, F32)

    x = xb_ref[...]
    a = (_silu(_dot(x, wg_ref[...])) * _dot(x, wu_ref[...])).astype(BF16)
    acc_ref[...] += _dot(a, wd_ref[...])

    @pl.when(j == pl.num_programs(1) - 1)
    def _():
        y_ref[...] = acc_ref[...]


def moe_experts(h, tok_buf, blk_e, wg, wu, wd, bm):
    N, D = h.shape
    rows = tok_buf.shape[0]
    E, _, F = wg.shape
    tf = F // 2 if F % (4 * LANES) == 0 else F
    return pl.pallas_call(
        functools.partial(_moe_kernel, bm=bm),
        grid_spec=pltpu.PrefetchScalarGridSpec(
            num_scalar_prefetch=2,
            grid=(rows // bm, F // tf),
            in_specs=[pl.BlockSpec(memory_space=pl.ANY),
                      pl.BlockSpec((None, D, tf), lambda i, j, be, tk: (be[i], 0, j)),
                      pl.BlockSpec((None, D, tf), lambda i, j, be, tk: (be[i], 0, j)),
                      pl.BlockSpec((None, tf, D), lambda i, j, be, tk: (be[i], j, 0))],
            out_specs=pl.BlockSpec((bm, D), lambda i, j, be, tk: (i, 0)),
            scratch_shapes=[pltpu.VMEM((bm, D), F32), pltpu.VMEM((bm, D), BF16), pltpu.VMEM((bm, D), F32),
                            pltpu.SemaphoreType.DMA(())],
        ),
        out_shape=jax.ShapeDtypeStruct((rows, D), F32),
        compiler_params=_cparams(("arbitrary", "arbitrary")),
        name="moe_experts",
    )(blk_e, tok_buf, h, wg, wu, wd)


def _moe_combine_kernel(pos_ref, y_hbm, gate_ref, x_ref, mod_ref, ng_ref, o_ref, ybuf, sem, *, tm, rows_per_batch):
    b = pl.program_id(0)
    i = pl.program_id(1)
    base = (b * rows_per_batch + i * tm) * TOP_K

    def start(r, c):
        for kk in range(TOP_K):
            pltpu.make_async_copy(y_hbm.at[pos_ref[base + r * TOP_K + kk]], ybuf.at[kk * tm + r], sem).start()
        return c

    lax.fori_loop(0, tm, start, 0, unroll=4)
    pltpu.make_async_copy(y_hbm.at[pl.ds(0, TOP_K * tm)], ybuf, sem).wait()
    gate = gate_ref[...]
    y = ybuf[0:tm, :] * gate[:, 0:1] + ybuf[tm:2 * tm, :] * gate[:, 1:2]
    o_ref[...] = x_ref[...] + mod_ref[5:6, :] * _rms(y, ng_ref[3:4, :])


def moe_combine(pos, y, gates, x, mod, ng):
    B, T, D = x.shape
    tm = _tile(T, 256)
    return pl.pallas_call(
        functools.partial(_moe_combine_kernel, tm=tm, rows_per_batch=T),
        grid_spec=pltpu.PrefetchScalarGridSpec(
            num_scalar_prefetch=1,
            grid=(B, T // tm),
            in_specs=[pl.BlockSpec(memory_space=pl.ANY),
                      pl.BlockSpec((None, tm, TOP_K), lambda b, i, p: (b, i, 0)),
                      pl.BlockSpec((None, tm, D), lambda b, i, p: (b, i, 0)),
                      pl.BlockSpec((None, 6, D), lambda b, i, p: (b, 0, 0)),
                      pl.BlockSpec(ng.shape, lambda b, i, p: (0, 0))],
            out_specs=pl.BlockSpec((None, tm, D), lambda b, i, p: (b, i, 0)),
            scratch_shapes=[pltpu.VMEM((TOP_K * tm, D), F32), pltpu.SemaphoreType.DMA(())],
        ),
        out_shape=jax.ShapeDtypeStruct((B, T, D), F32),
        compiler_params=_cparams(("arbitrary", "arbitrary")),
        name="moe_combine",
    )(pos, y, gates, x, mod, ng)


def _routing_tables(top_idx, bm):
    n = top_idx.shape[0]
    flat_e = top_idx.reshape(-1)
    onehot = (flat_e[:, None] == jnp.arange(N_EXPERTS, dtype=jnp.int32)[None, :]).astype(jnp.int32)
    csum = jnp.cumsum(onehot, axis=0)
    rank = jnp.sum(csum * onehot, axis=1) - 1
    counts = csum[-1]
    padded = (counts + bm - 1) // bm * bm
    pad_end = jnp.cumsum(padded)
    pad_start = pad_end - padded
    pos = jnp.sum(pad_start[None, :] * onehot, axis=1) + rank
    rows = n * TOP_K + N_EXPERTS * bm
    flat_tok = jnp.arange(n * TOP_K, dtype=jnp.int32) // TOP_K
    tok_buf = jnp.zeros((rows,), jnp.int32).at[pos].set(flat_tok)
    blk_start = jnp.arange(rows // bm, dtype=jnp.int32) * bm
    blk_e = jnp.minimum(jnp.searchsorted(pad_end, blk_start, side='right'), N_EXPERTS - 1).astype(jnp.int32)
    return pos.astype(jnp.int32), tok_buf, blk_e


def moe(x, mod, ng, w_router, wg, wu, wd, bm):
    B, T, D = x.shape
    h, top_idx, gates = router(x, mod, ng, w_router)
    pos, tok_buf, blk_e = _routing_tables(top_idx.reshape(B * T, TOP_K), bm)
    y = moe_experts(h.reshape(B * T, D), tok_buf, blk_e, wg, wu, wd, bm)
    return moe_combine(pos, y, gates, x, mod, ng)


def _prep_weights(p):
    w = {}
    w['l0'] = _prep_l0_weights(p['w_in_ab'][0], p['mla_w_qb'][0], p['mla_w_kvb'][0])
    w['qn'] = p['mla_q_norm'][0][None, :]
    w['kvn'] = p['mla_kv_norm'][0][None, :]
    w_out = p['w_out_ab'][0]
    na = MLA_HEADS * MLA_V
    g = SWA_HEADS // SWA_KV_HEADS
    D = w_out.shape[1]
    wa = jnp.pad(w_out[:na].reshape(MLA_HEADS, MLA_V, D), ((0, 0), (0, LANES - MLA_V), (0, 0)))
    w['wa'] = wa.reshape(MLA_HEADS * LANES, D).astype(BF16)
    w['wb'] = w_out[na:].reshape(SWA_KV_HEADS, g, SWA_HEAD_DIM, D).transpose(1, 0, 2, 3).reshape(-1, D).astype(BF16)
    w['ffn'] = tuple(p[k][0].astype(BF16) for k in ('ffn_w_gate', 'ffn_w_up', 'ffn_w_down'))
    w_in_c = p['w_in_c'][0]
    ncol = w_in_c.shape[1]
    w['win_c'] = jnp.pad(w_in_c, ((0, 0), (0, LANES - 2 * GLA_GATE_RANK))).astype(BF16)
    wgu = p['gla_w_gate_up'][0]
    nk = wgu.shape[-1]
    zero = jnp.zeros((GLA_GATE_RANK, nk), F32)
    wgate = jnp.concatenate([jnp.concatenate([wgu[0], zero], axis=1), jnp.concatenate([zero, wgu[1]], axis=1)], axis=0)
    w['wgate'] = jnp.pad(wgate, ((0, LANES - 2 * GLA_GATE_RANK), (0, 0))).astype(BF16)
    w['bgate'] = p['gla_b_gate_up'][0].reshape(1, 2 * nk)
    w['nk'] = nk
    w['nv'] = (ncol - 2 * nk - 2 * GLA_GATE_RANK) // 2
    w['out_norm'] = p['gla_out_norm'][0][None, :]
    w['w_out_c'] = p['w_out_c'][0].astype(BF16)
    w['router'] = p['moe_router'][0]
    w['moe'] = tuple(p[k][0].astype(BF16) for k in ('moe_w_gate', 'moe_w_up', 'moe_w_down'))
    w['swa_bias'] = swa_bias(p['rel_bias'])
    w['sink'] = p['swa_sink'][0]
    return w


def _trunk(x, c, p, w, moe_bm):
    B, T, D = x.shape
    tables = _rope_tables(T)
    mod = ada_mod(c, p['w_ada'], p['b_ada'], 0)
    ng = p['norm_gain'][0]
    q, k, v, sq, sk, sv = l0_proj(x, mod, ng, w['l0'][0], w['qn'], w['kvn'], *w['l0'][1:], tables)
    a = mla_attention(q, k, v)
    b = swa_attention(sq, sk, sv, w['sink'], w['swa_bias'])
    x = out_res(a, b, w['wa'], w['wb'], x, mod, ng)
    x = ffn(x, mod, ng, *w['ffn'])
    mod = ada_mod(c, p['w_ada'], p['b_ada'], 1)
    ng = p['norm_gain'][1]
    q, k, v, r, gf, gb = l1_proj(x, mod, ng, w['win_c'], w['wgate'], w['bgate'], w['nk'], w['nv'])
    o_f = gla_scan(q, k, gf, v, reverse=False)
    o_b = gla_scan(q, k, gb, v, reverse=True)
    x = gla_out(o_f, o_b, r, w['out_norm'], w['w_out_c'], x, mod, ng)
    return moe(x, mod, ng, w['router'], *w['moe'], moe_bm)


def kernel(x_prompt, x_sample, c_prompt, c_sample, norm_gain, w_ada, b_ada, rel_bias, w_in_ab, mla_q_norm, mla_kv_norm, mla_w_qb, mla_w_kvb, swa_sink, w_out_ab, ffn_w_gate, ffn_w_up, ffn_w_down, w_in_c, gla_w_gate_up, gla_b_gate_up, gla_out_norm, w_out_c, moe_router, moe_w_gate, moe_w_up, moe_w_down):
    p = dict(norm_gain=norm_gain, w_ada=w_ada, b_ada=b_ada, rel_bias=rel_bias, w_in_ab=w_in_ab,
             mla_q_norm=mla_q_norm, mla_kv_norm=mla_kv_norm, mla_w_qb=mla_w_qb, mla_w_kvb=mla_w_kvb,
             swa_sink=swa_sink, w_out_ab=w_out_ab, ffn_w_gate=ffn_w_gate, ffn_w_up=ffn_w_up,
             ffn_w_down=ffn_w_down, w_in_c=w_in_c, gla_w_gate_up=gla_w_gate_up, gla_b_gate_up=gla_b_gate_up,
             gla_out_norm=gla_out_norm, w_out_c=w_out_c, moe_router=moe_router, moe_w_gate=moe_w_gate,
             moe_w_up=moe_w_up, moe_w_down=moe_w_down)
    w = _prep_weights(p)
    bm = 512
    return (_trunk(x_prompt, c_prompt, p, w, bm), _trunk(x_sample, c_sample, p, w, bm))
```

```python
import functools
import math

import jax
import jax.numpy as jnp
from jax import lax
from jax.experimental import pallas as pl
from jax.experimental.pallas import tpu as pltpu

F32 = jnp.float32
BF16 = jnp.bfloat16

MLA_HEADS = 8
MLA_NOPE = 64
MLA_ROPE = 32
MLA_V = 64
MLA_Q_LORA = 384
MLA_KV_LORA = 256
ROPE_THETA = 10000.0
SWA_HEADS = 8
SWA_KV_HEADS = 2
SWA_HEAD_DIM = 64
WINDOW = 128
REL_BUCKETS = 32
REL_MAX_DIST = 128
GLA_HEADS = 4
GLA_GATE_RANK = 16
GLA_GATE_NORM = 16.0
GLA_CHUNK = 64
N_EXPERTS = 8
TOP_K = 2
EPS = 1e-6

LANES = 128
VMEM_LIMIT = 56 * 1024 * 1024


def _cparams(sem, vmem=VMEM_LIMIT):
    return pltpu.CompilerParams(dimension_semantics=sem, vmem_limit_bytes=vmem)


def _tile(n, pref):
    if n <= pref:
        return n
    t = pref
    while n % t:
        t -= 8
    return t


def _rms(x, g):
    return x * lax.rsqrt(jnp.mean(x * x, axis=-1, keepdims=True) + EPS) * g


def _modulate(x, g, shift, scale):
    return _rms(x, g) * (1.0 + scale) + shift


def _silu(x):
    return x / (1.0 + jnp.exp(-x))


def _dot(a, b):
    return jnp.dot(a, b, preferred_element_type=F32)


def _dot_nt(a, b):
    return lax.dot_general(a, b, (((1,), (1,)), ((), ())), preferred_element_type=F32)


def _dot_tn(a, b):
    return lax.dot_general(a, b, (((0,), (0,)), ((), ())), preferred_element_type=F32)


def _ada_kernel(c_ref, w_ref, b_ref, o_ref):
    s = _silu(c_ref[...]).astype(BF16)
    o_ref[...] = _dot(s, w_ref[...].astype(BF16)) + b_ref[...]


def ada_mod(c, w_ada, b_ada, layer):
    B, D = c.shape
    N = w_ada.shape[-1]
    Bp = -(-B // 16) * 16
    cp = jnp.pad(c, ((0, Bp - B), (0, 0)))
    tn = _tile(N, 1536)
    out = pl.pallas_call(
        _ada_kernel,
        grid=(N // tn,),
        in_specs=[pl.BlockSpec((Bp, D), lambda j: (0, 0)),
                  pl.BlockSpec((None, D, tn), lambda j: (layer, 0, j)),
                  pl.BlockSpec((None, 1, tn), lambda j: (layer, 0, j))],
        out_specs=pl.BlockSpec((Bp, tn), lambda j: (0, j)),
        out_shape=jax.ShapeDtypeStruct((Bp, N), F32),
        compiler_params=_cparams(("parallel",)),
        name="ada_mod",
    )(cp, w_ada, b_ada.reshape(b_ada.shape[0], 1, N))
    return out[:B].reshape(B, 6, D)


def _l0_proj_kernel(x_ref, mod_ref, ng_ref, win_ref, qn_ref, kvn_ref, wq_ref, wqs_ref, wk_ref, wv_ref,
                    cq_ref, sq_ref, ck_ref, sk_ref,
                    q_ref, k_ref, v_ref, sq_out_ref, sk_out_ref, sv_out_ref):
    h = _modulate(x_ref[...], ng_ref[0:1, :], mod_ref[0:1, :], mod_ref[1:2, :]).astype(BF16)
    p = _dot(h, win_ref[...])
    o1 = MLA_Q_LORA
    o2 = o1 + MLA_KV_LORA
    nq = _rms(p[:, :o1], qn_ref[...]).astype(BF16)
    q = _dot(nq, wq_ref[...])
    qs = _dot(nq, wqs_ref[...])
    cq = jnp.concatenate([cq_ref[...]] * MLA_HEADS, axis=1)
    sq = jnp.concatenate([sq_ref[...]] * MLA_HEADS, axis=1)
    q_ref[...] = (q * cq + qs * sq).astype(q_ref.dtype)
    nkv = _rms(p[:, o1:o2], kvn_ref[...]).astype(BF16)
    kpe = p[:, o2:o2 + LANES] * ck_ref[...] + p[:, o2 + LANES:o2 + 2 * LANES] * sk_ref[...]
    k = _dot(nkv, wk_ref[...]) + jnp.concatenate([kpe] * MLA_HEADS, axis=1)
    k_ref[...] = k.astype(k_ref.dtype)
    v = _dot(nkv, wv_ref[...])
    lane = lax.broadcasted_iota(jnp.int32, v.shape, 1)
    v_ref[...] = jnp.where((lane & (LANES - 1)) == MLA_V, 1.0, v).astype(v_ref.dtype)
    o3 = o2 + 2 * LANES
    o4 = o3 + SWA_HEADS * LANES
    sq_out_ref[...] = (p[:, o3:o4] * (SWA_HEAD_DIM ** -0.5)).astype(sq_out_ref.dtype)
    sk_out_ref[...] = p[:, o4:o4 + LANES].astype(sk_out_ref.dtype)
    sv_out_ref[...] = p[:, o4 + LANES:o4 + 2 * LANES].astype(sv_out_ref.dtype)


def _prep_l0_weights(w_in, w_qb, w_kvb):
    D = w_in.shape[0]
    o1 = MLA_Q_LORA
    o2 = o1 + MLA_KV_LORA
    o2r = o2 + MLA_ROPE
    half = MLA_ROPE // 2
    padl = lambda w, l, r: jnp.pad(w, ((0, 0), (l, r)))
    kpe = w_in[:, o2:o2r]
    kpe_sw = jnp.concatenate([-kpe[:, half:], kpe[:, :half]], axis=1)
    o3 = o2r + SWA_HEADS * SWA_HEAD_DIM
    o4 = o3 + SWA_KV_HEADS * SWA_HEAD_DIM
    g = SWA_HEADS // SWA_KV_HEADS
    swa_q = []
    for h in range(SWA_HEADS):
        hk = h // g
        blk = w_in[:, o2r + h * SWA_HEAD_DIM: o2r + (h + 1) * SWA_HEAD_DIM]
        swa_q.append(padl(blk, hk * SWA_HEAD_DIM, LANES - (hk + 1) * SWA_HEAD_DIM))
    win = jnp.concatenate(
        [w_in[:, :o2], padl(kpe, MLA_NOPE, LANES - MLA_NOPE - MLA_ROPE),
         padl(kpe_sw, MLA_NOPE, LANES - MLA_NOPE - MLA_ROPE)] + swa_q + [w_in[:, o3:o4], w_in[:, o4:]],
        axis=1).astype(BF16)
    dqk = MLA_NOPE + MLA_ROPE
    wq3 = w_qb.reshape(o1, MLA_HEADS, dqk)
    wq = jnp.pad(wq3, ((0, 0), (0, 0), (0, LANES - dqk))).reshape(o1, MLA_HEADS * LANES).astype(BF16)
    x1 = wq3[..., MLA_NOPE:MLA_NOPE + half]
    x2 = wq3[..., MLA_NOPE + half:]
    z = jnp.zeros_like
    wqs = jnp.concatenate([z(wq3[..., :MLA_NOPE]), -x2, x1, z(wq3[..., :LANES - dqk])], axis=-1)
    wqs = wqs.reshape(o1, MLA_HEADS * LANES).astype(BF16)
    wkv3 = w_kvb.reshape(MLA_KV_LORA, MLA_HEADS, MLA_NOPE + MLA_V)
    wk = jnp.pad(wkv3[..., :MLA_NOPE], ((0, 0), (0, 0), (0, LANES - MLA_NOPE)))
    wk = wk.reshape(MLA_KV_LORA, MLA_HEADS * LANES).astype(BF16)
    wv = jnp.pad(wkv3[..., MLA_NOPE:], ((0, 0), (0, 0), (0, LANES - MLA_V)))
    wv = wv.reshape(MLA_KV_LORA, MLA_HEADS * LANES).astype(BF16)
    return win, wq, wqs, wk, wv


def _rope_tables(T):
    half = MLA_ROPE // 2
    inv = 1.0 / (ROPE_THETA ** (jnp.arange(half, dtype=F32) / half))
    ang = jnp.arange(T, dtype=F32)[:, None] * inv[None, :]
    cos, sin = jnp.cos(ang), jnp.sin(ang)
    s = (MLA_NOPE + MLA_ROPE) ** -0.5 * math.log2(math.e)
    one = jnp.ones((T, MLA_NOPE), F32)
    zl = jnp.zeros((T, MLA_NOPE), F32)
    zr = jnp.zeros((T, LANES - MLA_NOPE - MLA_ROPE), F32)
    cq = jnp.concatenate([one * s, cos * s, cos * s, zr], axis=1)
    sq = jnp.concatenate([zl, sin * s, sin * s, zr], axis=1)
    ck = jnp.concatenate([zl, cos, cos, zr], axis=1)
    sk = jnp.concatenate([zl, sin, sin, zr], axis=1)
    return cq, sq, ck, sk


def l0_proj(x, mod, ng, win, qn, kvn, wq, wqs, wk, wv, tables):
    B, T, D = x.shape
    tm = _tile(T, 512)
    NP = win.shape[1]
    HQ = MLA_HEADS * LANES
    full = lambda a: pl.BlockSpec(a.shape, lambda b, i: (0,) * a.ndim)
    row = lambda n: pl.BlockSpec((None, tm, n), lambda b, i: (b, i, 0))
    tab = pl.BlockSpec((tm, LANES), lambda b, i: (i, 0))
    outs = [(HQ, BF16), (HQ, BF16), (HQ, BF16), (SWA_HEADS * LANES, BF16), (LANES, BF16), (LANES, BF16)]
    return pl.pallas_call(
        _l0_proj_kernel,
        grid=(B, T // tm),
        in_specs=[row(D), pl.BlockSpec((None, 6, D), lambda b, i: (b, 0, 0)), full(ng), full(win), full(qn),
                  full(kvn), full(wq), full(wqs), full(wk), full(wv), tab, tab, tab, tab],
        out_specs=[row(n) for n, _ in outs],
        out_shape=[jax.ShapeDtypeStruct((B, T, n), dt) for n, dt in outs],
        compiler_params=_cparams(("parallel", "parallel")),
        name="l0_proj",
    )(x, mod, ng, win, qn, kvn, wq, wqs, wk, wv, *tables)


def _mla_attn_kernel(q_ref, k_ref, v_ref, o_ref, m_ref, acc_ref, *, tk):
    T = k_ref.shape[0]
    nk = T // tk
    m_ref[...] = jnp.full(m_ref.shape, -jnp.inf, F32)
    acc_ref[...] = jnp.zeros(acc_ref.shape, F32)
    q = q_ref[...]
    per_trip = next(u for u in (4, 2, 1) if nk % u == 0)

    def body(j, carry):
        r0 = [pl.multiple_of((j * per_trip + u) * tk, tk) for u in range(per_trip)]
        s = [_dot_nt(q, k_ref[pl.ds(r, tk), :]) for r in r0]
        m = m_ref[...]
        acc = acc_ref[...]
        for u in range(per_trip):
            m_new = jnp.maximum(m, jnp.max(s[u], axis=-1, keepdims=True))
            p = jnp.exp2(s[u] - jnp.tile(m_new, (1, tk // LANES)))
            acc = jnp.exp2(m - m_new) * acc + _dot(p.astype(BF16), v_ref[pl.ds(r0[u], tk), :])
            m = m_new
        m_ref[...] = m
        acc_ref[...] = acc
        return carry

    lax.fori_loop(0, nk // per_trip, body, 0)
    acc = acc_ref[...]
    o_ref[...] = (acc / acc[:, MLA_V:MLA_V + 1]).astype(o_ref.dtype)


def mla_attention(q, k, v):
    B, T, _ = q.shape
    tq = _tile(T, 512)
    tk = _tile(T, 1024)
    qspec = pl.BlockSpec((None, tq, LANES), lambda b, h, i: (b, i, h))
    kspec = pl.BlockSpec((None, T, LANES), lambda b, h, i: (b, 0, h))
    return pl.pallas_call(
        functools.partial(_mla_attn_kernel, tk=tk),
        grid=(B, MLA_HEADS, T // tq),
        in_specs=[qspec, kspec, kspec],
        out_specs=qspec,
        out_shape=jax.ShapeDtypeStruct((B, T, MLA_HEADS * LANES), BF16),
        scratch_shapes=[pltpu.VMEM((tq, LANES), F32), pltpu.VMEM((tq, LANES), F32)],
        compiler_params=_cparams(("parallel", "parallel", "parallel")),
        name="mla_attention",
    )(q, k, v)


def _t5_bucket(rel):
    nb = REL_BUCKETS // 2
    max_exact = nb // 2
    n = jnp.abs(rel)
    big = max_exact + (jnp.log(jnp.maximum(n, 1).astype(F32) / max_exact)
                       / math.log(REL_MAX_DIST / max_exact) * (nb - max_exact)).astype(jnp.int32)
    big = jnp.minimum(big, nb - 1)
    return jnp.where(rel > 0, nb, 0) + jnp.where(n < max_exact, n, big)


def _swa_bias_kernel(rb_ref, bucket_ref, o_ref):
    h = pl.program_id(0)
    bucket = bucket_ref[...]
    acc = jnp.zeros(bucket.shape, F32)
    for b in range(REL_BUCKETS):
        acc = jnp.where(bucket == b, rb_ref[b * SWA_HEADS + h], acc)
    qi = lax.broadcasted_iota(jnp.int32, bucket.shape, 0)
    kj = lax.broadcasted_iota(jnp.int32, bucket.shape, 1)
    rel = kj - WINDOW - qi
    o_ref[...] = jnp.where(jnp.abs(rel) <= WINDOW, acc, -jnp.inf)


def swa_bias(rel_bias):
    W = WINDOW
    qi = jnp.arange(W, dtype=jnp.int32)[:, None]
    kj = jnp.arange(3 * W, dtype=jnp.int32)[None, :]
    bucket = _t5_bucket(kj - W - qi)
    return pl.pallas_call(
        _swa_bias_kernel,
        grid=(SWA_HEADS,),
        in_specs=[pl.BlockSpec(memory_space=pltpu.SMEM), pl.BlockSpec((W, 3 * W), lambda h: (0, 0))],
        out_specs=pl.BlockSpec((None, W, 3 * W), lambda h: (h, 0, 0)),
        out_shape=jax.ShapeDtypeStruct((SWA_HEADS, W, 3 * W), F32),
        name="swa_bias",
    )(rel_bias.reshape(-1), bucket)


def _swa_kernel(sink_ref, q_ref, kp_ref, kc_ref, kn_ref, vp_ref, vc_ref, vn_ref, bias_ref, o_ref):
    n = pl.program_id(1)
    nb = pl.num_programs(1)
    W = WINDOW
    kband = jnp.concatenate([kp_ref[...], kc_ref[...], kn_ref[...]], axis=0)
    vband = jnp.concatenate([vp_ref[...], vc_ref[...], vn_ref[...]], axis=0)
    col = lax.broadcasted_iota(jnp.int32, (W, 3 * W), 1)
    lo = jnp.where(n == 0, W, 0)
    hi = jnp.where(n == nb - 1, 2 * W, 3 * W)
    inside = (col >= lo) & (col < hi)
    lane = lax.broadcasted_iota(jnp.int32, (W, LANES), 1)
    G = SWA_HEADS // SWA_KV_HEADS
    heads = range(SWA_HEADS)
    s = [_dot_nt(q_ref[:, h * LANES:(h + 1) * LANES], kband) for h in heads]
    s = [jnp.where(inside, s[h] + bias_ref[h], -jnp.inf) for h in heads]
    m = [jnp.maximum(jnp.max(s[h], axis=-1, keepdims=True), sink_ref[h]) for h in heads]
    p = [jnp.exp(s[h] - m[h]) for h in heads]
    den = [jnp.sum(p[h], axis=-1, keepdims=True) + jnp.exp(sink_ref[h] - m[h]) for h in heads]
    o = [_dot(p[h].astype(BF16), vband) for h in heads]
    o = [o[h] / den[h] for h in heads]
    outs = [jnp.where(lane < SWA_HEAD_DIM, o[g], o[G + g]) for g in range(G)]
    o_ref[...] = jnp.concatenate(outs, axis=1).astype(o_ref.dtype)


def swa_attention(q, k, v, sink, bias):
    B, T, _ = q.shape
    W = WINDOW
    nb = T // W
    G = SWA_HEADS // SWA_KV_HEADS
    prev = pl.BlockSpec((None, W, LANES), lambda b, n: (b, jnp.maximum(n - 1, 0), 0))
    cur = pl.BlockSpec((None, W, LANES), lambda b, n: (b, n, 0))
    nxt = pl.BlockSpec((None, W, LANES), lambda b, n: (b, jnp.minimum(n + 1, nb - 1), 0))
    return pl.pallas_call(
        _swa_kernel,
        grid=(B, nb),
        in_specs=[pl.BlockSpec(memory_space=pltpu.SMEM),
                  pl.BlockSpec((None, W, SWA_HEADS * LANES), lambda b, n: (b, n, 0)),
                  prev, cur, nxt, prev, cur, nxt,
                  pl.BlockSpec(bias.shape, lambda b, n: (0, 0, 0))],
        out_specs=pl.BlockSpec((None, W, G * LANES), lambda b, n: (b, n, 0)),
        out_shape=jax.ShapeDtypeStruct((B, T, G * LANES), BF16),
        compiler_params=_cparams(("parallel", "parallel")),
        name="swa_attention",
    )(sink, q, k, k, k, v, v, v, bias)


def _out_res_kernel(a_ref, b_ref, wa_ref, wb_ref, x_ref, mod_ref, ng_ref, o_ref, *, gate_row, gain_row):
    y = _dot(a_ref[...], wa_ref[...]) + _dot(b_ref[...], wb_ref[...])
    o_ref[...] = x_ref[...] + mod_ref[gate_row:gate_row + 1, :] * _rms(y, ng_ref[gain_row:gain_row + 1, :])


def out_res(a, b, wa, wb, x, mod, ng):
    B, T, D = x.shape
    tm = _tile(T, 512)
    full = lambda w: pl.BlockSpec(w.shape, lambda bb, i: (0,) * w.ndim)
    row = lambda n: pl.BlockSpec((None, tm, n), lambda bb, i: (bb, i, 0))
    return pl.pallas_call(
        functools.partial(_out_res_kernel, gate_row=2, gain_row=1),
        grid=(B, T // tm),
        in_specs=[row(a.shape[-1]), row(b.shape[-1]), full(wa), full(wb), row(D),
                  pl.BlockSpec((None, 6, D), lambda bb, i: (bb, 0, 0)), full(ng)],
        out_specs=row(D),
        out_shape=jax.ShapeDtypeStruct((B, T, D), F32),
        compiler_params=_cparams(("parallel", "parallel")),
        name="out_res",
    )(a, b, wa, wb, x, mod, ng)


def _ffn_kernel(x_ref, mod_ref, ng_ref, wg_ref, wu_ref, wd_ref, o_ref):
    x = x_ref[...]
    h = _modulate(x, ng_ref[2:3, :], mod_ref[3:4, :], mod_ref[4:5, :]).astype(BF16)
    a = (_silu(_dot(h, wg_ref[...])) * _dot(h, wu_ref[...])).astype(BF16)
    o_ref[...] = x + mod_ref[5:6, :] * _rms(_dot(a, wd_ref[...]), ng_ref[3:4, :])


def ffn(x, mod, ng, wg, wu, wd):
    B, T, D = x.shape
    tm = _tile(T, 512)
    row = pl.BlockSpec((None, tm, D), lambda b, i: (b, i, 0))
    resident = lambda w: pl.BlockSpec(w.shape, lambda b, i: (0, 0), pipeline_mode=pl.Buffered(1))
    return pl.pallas_call(
        _ffn_kernel,
        grid=(B, T // tm),
        in_specs=[row, pl.BlockSpec((None, 6, D), lambda b, i: (b, 0, 0)),
                  pl.BlockSpec(ng.shape, lambda b, i: (0, 0)), resident(wg), resident(wu), resident(wd)],
        out_specs=row,
        out_shape=jax.ShapeDtypeStruct((B, T, D), F32),
        compiler_params=_cparams(("parallel", "parallel")),
        name="ffn",
    )(x, mod, ng, wg, wu, wd)


def _l1_proj_kernel(x_ref, mod_ref, ng_ref, win_ref, wgate_ref, bgate_ref,
                    q_ref, k_ref, v_ref, r_ref, gf_ref, gb_ref):
    h = _modulate(x_ref[...], ng_ref[0:1, :], mod_ref[0:1, :], mod_ref[1:2, :]).astype(BF16)
    p = _dot(h, win_ref[...])
    nk = q_ref.shape[-1]
    nv = v_ref.shape[-1]
    dk = nk // GLA_HEADS
    q_ref[...] = p[:, :nk] * (dk ** -0.5)
    k_ref[...] = p[:, nk:2 * nk]
    v_ref[...] = p[:, 2 * nk:2 * nk + nv].astype(v_ref.dtype)
    r_ref[...] = p[:, 2 * nk + nv:2 * nk + 2 * nv].astype(r_ref.dtype)
    gd = p[:, 2 * nk + 2 * nv:].astype(BF16)
    pre = _dot(gd, wgate_ref[...]) + bgate_ref[...]
    logsig = jnp.minimum(pre, 0.0) - jnp.log1p(jnp.exp(-jnp.abs(pre)))
    g = logsig / GLA_GATE_NORM
    gf_ref[...] = g[:, :nk]
    gb_ref[...] = g[:, nk:]


def l1_proj(x, mod, ng, win, wgate, bgate, nk, nv):
    B, T, D = x.shape
    tm = _tile(T, 512)
    full = lambda a: pl.BlockSpec(a.shape, lambda b, i: (0,) * a.ndim)
    row = lambda n: pl.BlockSpec((None, tm, n), lambda b, i: (b, i, 0))
    outs = [(nk, F32), (nk, F32), (nv, BF16), (nv, BF16), (nk, F32), (nk, F32)]
    return pl.pallas_call(
        _l1_proj_kernel,
        grid=(B, T // tm),
        in_specs=[row(D), pl.BlockSpec((None, 6, D), lambda b, i: (b, 0, 0)), full(ng), full(win), full(wgate),
                  full(bgate)],
        out_specs=[row(n) for n, _ in outs],
        out_shape=[jax.ShapeDtypeStruct((B, T, n), dt) for n, dt in outs],
        compiler_params=_cparams(("parallel", "parallel")),
        name="l1_proj",
    )(x, mod, ng, win, wgate, bgate)


def _gla_kernel(q_ref, k_ref, g_ref, v_ref, o_ref, st_ref, *, reverse):
    t = pl.program_id(2)
    L = GLA_CHUNK
    nchunk = q_ref.shape[0] // L

    @pl.when(t == 0)
    def _():
        st_ref[...] = jnp.zeros(st_ref.shape, F32)

    ri = lax.broadcasted_iota(jnp.int32, (L, L), 0)
    ci = lax.broadcasted_iota(jnp.int32, (L, L), 1)
    keep = (ci >= ri) if reverse else (ci <= ri)
    order = list(range(nchunk - 1, -1, -1) if reverse else range(nchunk))
    chunks = [slice(c * L, (c + 1) * L) for c in range(nchunk)]

    b = g_ref[...]
    n = b.shape[0]
    pos = lax.broadcasted_iota(jnp.int32, b.shape, 0) & (L - 1)
    step = 1
    while step < L:
        if reverse:
            b = b + jnp.where(pos < L - step, pltpu.roll(b, n - step, axis=0), 0.0)
        else:
            b = b + jnp.where(pos >= step, pltpu.roll(b, step, axis=0), 0.0)
        step *= 2
    ends = [b[c * L:c * L + 1, :] if reverse else b[(c + 1) * L - 1:(c + 1) * L, :] for c in range(nchunk)]
    b_end = jnp.concatenate([jnp.broadcast_to(e, (L, e.shape[1])) for e in ends], axis=0)
    k = k_ref[...]
    qe = (q_ref[...] * jnp.exp(b)).astype(BF16)
    ke = (k * jnp.exp(-b)).astype(BF16)
    kd = (k * jnp.exp(b_end - b)).astype(BF16)
    v = v_ref[...]
    a = [_dot_nt(qe[r], ke[r]) for r in chunks]
    a = [jnp.where(keep, x, 0.0).astype(BF16) for x in a]
    intra = [_dot(a[c], v[chunks[c]]) for c in range(nchunk)]
    upd = [_dot_tn(v[r], kd[r]) for r in chunks]
    st = st_ref[...]
    outs = [None] * nchunk
    for c in order:
        outs[c] = (intra[c] + _dot_nt(qe[chunks[c]], st.astype(BF16))).astype(o_ref.dtype)
        st = st * jnp.exp(ends[c]) + upd[c]
    st_ref[...] = st
    o_ref[...] = jnp.concatenate(outs, axis=0)


def gla_scan(q, k, g, v, reverse):
    B, T, nk = q.shape
    nv = v.shape[-1]
    dk, dv = nk // GLA_HEADS, nv // GLA_HEADS
    tc = _tile(T, 512)
    nt = T // tc
    blk = (lambda t: nt - 1 - t) if reverse else (lambda t: t)
    qspec = pl.BlockSpec((None, tc, dk), lambda b, h, t: (b, blk(t), h))
    vspec = pl.BlockSpec((None, tc, dv), lambda b, h, t: (b, blk(t), h))
    return pl.pallas_call(
        functools.partial(_gla_kernel, reverse=reverse),
        grid=(B, GLA_HEADS, nt),
        in_specs=[qspec, qspec, qspec, vspec],
        out_specs=vspec,
        out_shape=jax.ShapeDtypeStruct((B, T, nv), F32),
        scratch_shapes=[pltpu.VMEM((dv, dk), F32)],
        compiler_params=_cparams(("parallel", "parallel", "arbitrary")),
        name="gla_bwd" if reverse else "gla_fwd",
    )(q, k, g, v)


def _gla_out_kernel(of_ref, ob_ref, r_ref, on_ref, w_ref, x_ref, mod_ref, ng_ref, o_ref):
    o = of_ref[...] + ob_ref[...]
    dv = on_ref.shape[-1]
    parts = [_rms(o[:, h * dv:(h + 1) * dv], on_ref[...]) for h in range(GLA_HEADS)]
    on = jnp.concatenate(parts, axis=1)
    y = _dot((on * _silu(r_ref[...].astype(F32))).astype(BF16), w_ref[...])
    o_ref[...] = x_ref[...] + mod_ref[2:3, :] * _rms(y, ng_ref[1:2, :])


def gla_out(o_f, o_b, r, out_norm, w_out, x, mod, ng):
    B, T, D = x.shape
    tm = _tile(T, 512)
    nv = o_f.shape[-1]
    full = lambda w: pl.BlockSpec(w.shape, lambda bb, i: (0,) * w.ndim)
    row = lambda n: pl.BlockSpec((None, tm, n), lambda bb, i: (bb, i, 0))
    return pl.pallas_call(
        _gla_out_kernel,
        grid=(B, T // tm),
        in_specs=[row(nv), row(nv), row(nv), full(out_norm), full(w_out), row(D),
                  pl.BlockSpec((None, 6, D), lambda bb, i: (bb, 0, 0)), full(ng)],
        out_specs=row(D),
        out_shape=jax.ShapeDtypeStruct((B, T, D), F32),
        compiler_params=_cparams(("parallel", "parallel")),
        name="gla_out",
    )(o_f, o_b, r, out_norm, w_out, x, mod, ng)


def _store_row_tiles(ref, x):
    m = x.shape[0]
    nc = x.shape[1] // LANES
    for c in range(nc):
        ref[pl.ds(c, m, stride=nc), :] = x[:, c * LANES:(c + 1) * LANES]


def _load_row_tiles(ref, nc):
    m = ref.shape[0] // nc
    return jnp.concatenate([ref[pl.ds(c, m, stride=nc), :] for c in range(nc)], axis=1)


def _router_kernel(x_ref, mod_ref, ng_ref, whi_ref, wlo_ref, h_ref, idx_ref, gate_ref):
    h = _modulate(x_ref[...], ng_ref[2:3, :], mod_ref[3:4, :], mod_ref[4:5, :])
    _store_row_tiles(h_ref, h)
    h_hi = h.astype(BF16)
    h_lo = (h - h_hi.astype(F32)).astype(BF16)
    logits = _dot(h_hi, whi_ref[...]) + _dot(h_lo, whi_ref[...]) + _dot(h_hi, wlo_ref[...])
    lane = lax.broadcasted_iota(jnp.int32, logits.shape, 1)
    logits = jnp.where(lane < N_EXPERTS, logits, -jnp.inf)
    m1 = jnp.max(logits, axis=-1, keepdims=True)
    i1 = jnp.min(jnp.where(logits == m1, lane, LANES), axis=-1, keepdims=True)
    rest = jnp.where(lane == i1, -jnp.inf, logits)
    m2 = jnp.max(rest, axis=-1, keepdims=True)
    i2 = jnp.min(jnp.where(rest == m2, lane, LANES), axis=-1, keepdims=True)
    e = jnp.exp(m2 - m1)
    idx_ref[...] = jnp.concatenate([i1, i2], axis=1)
    gate_ref[...] = jnp.concatenate([1.0 / (1.0 + e), e / (1.0 + e)], axis=1)


def router(x, mod, ng, w_router):
    B, T, D = x.shape
    tm = _tile(T, 512)
    wp = jnp.pad(w_router, ((0, 0), (0, LANES - N_EXPERTS)))
    whi = wp.astype(BF16)
    wlo = (wp - whi.astype(F32)).astype(BF16)
    full = lambda w: pl.BlockSpec(w.shape, lambda bb, i: (0,) * w.ndim)
    row = lambda n: pl.BlockSpec((None, tm, n), lambda bb, i: (bb, i, 0))
    return pl.pallas_call(
        _router_kernel,
        grid=(B, T // tm),
        in_specs=[row(D), pl.BlockSpec((None, 6, D), lambda bb, i: (bb, 0, 0)), full(ng), full(whi), full(wlo)],
        out_specs=[pl.BlockSpec((None, tm * (D // LANES), LANES), lambda bb, i: (bb, i, 0)), row(TOP_K), row(TOP_K)],
        out_shape=[jax.ShapeDtypeStruct((B, T * (D // LANES), LANES), F32),
                   jax.ShapeDtypeStruct((B, T, TOP_K), jnp.int32), jax.ShapeDtypeStruct((B, T, TOP_K), F32)],
        compiler_params=_cparams(("parallel", "parallel")),
        name="moe_router",
    )(x, mod, ng, whi, wlo)


def _moe_kernel(blk_e_ref, tok_ref, dst_ref, h_hbm, wg_ref, wu_ref, wd_ref, out_hbm,
                xbuf, xb_ref, acc_ref, ybuf, gsem, ssem, *, bm, nj):
    i = pl.program_id(0)
    j = pl.program_id(1)
    nb = pl.num_programs(0)
    slot = lax.rem(i, 2)
    other = 1 - slot
    per = bm // nj

    nc = xbuf.shape[1] // bm

    def tile(row):
        return pl.ds(pl.multiple_of(row * nc, nc), nc)

    def gather(blk, r, s):
        return pltpu.make_async_copy(h_hbm.at[tile(tok_ref[blk * bm + r])], xbuf.at[s, tile(r)], gsem.at[s])

    def scatter(blk, r, s):
        return pltpu.make_async_copy(ybuf.at[s, tile(r)], out_hbm.at[tile(dst_ref[(blk + 1) * bm + r])], ssem.at[s])

    def wait_rows(buf, sem, s):
        pltpu.make_async_copy(h_hbm.at[pl.ds(0, bm * nc)], buf.at[s], sem.at[s]).wait()

    @pl.when((i == 0) & (j == 0))
    def _():
        def start(r, c):
            gather(0, r, 0).start()
            return c

        lax.fori_loop(0, bm, start, 0, unroll=8)
        ybuf[1] = jnp.zeros(ybuf.shape[1:], F32)

    @pl.when(j == 0)
    def _():
        wait_rows(xbuf, gsem, slot)
        xb_ref[...] = _load_row_tiles(xbuf.at[slot], nc).astype(BF16)
        acc_ref[...] = jnp.zeros(acc_ref.shape, F32)

    nxt = jnp.minimum(i + 1, nb - 1)
    for u in range(per):
        r = j * per + u
        gather(nxt, r, other).start()
        scatter(i - 1, r, other).start()

    x = xb_ref[...]
    a = (_silu(_dot(x, wg_ref[...])) * _dot(x, wu_ref[...])).astype(BF16)
    acc_ref[...] += _dot(a, wd_ref[...])

    @pl.when(j == nj - 1)
    def _():
        @pl.when(i >= 1)
        def _():
            wait_rows(ybuf, ssem, slot)

        _store_row_tiles(ybuf.at[slot], acc_ref[...])

        @pl.when(i == nb - 1)
        def _():
            def start(r, c):
                scatter(i, r, slot).start()
                return c

            lax.fori_loop(0, bm, start, 0, unroll=8)
            wait_rows(ybuf, ssem, slot)
            wait_rows(ybuf, ssem, other)
            wait_rows(xbuf, gsem, other)


def moe_experts(h, tok_buf, dst, blk_e, wg, wu, wd, bm, out_rows):
    D = wg.shape[1]
    C = D // LANES
    rows = tok_buf.shape[0]
    E, _, F = wg.shape
    nj = 2 if F % (4 * LANES) == 0 and bm % 2 == 0 else 1
    tf = F // nj
    return pl.pallas_call(
        functools.partial(_moe_kernel, bm=bm, nj=nj),
        grid_spec=pltpu.PrefetchScalarGridSpec(
            num_scalar_prefetch=3,
            grid=(rows // bm, nj),
            in_specs=[pl.BlockSpec(memory_space=pl.ANY),
                      pl.BlockSpec((None, D, tf), lambda i, j, be, tk, ds: (be[i], 0, j)),
                      pl.BlockSpec((None, D, tf), lambda i, j, be, tk, ds: (be[i], 0, j)),
                      pl.BlockSpec((None, tf, D), lambda i, j, be, tk, ds: (be[i], j, 0))],
            out_specs=pl.BlockSpec(memory_space=pl.ANY),
            scratch_shapes=[pltpu.VMEM((2, bm * C, LANES), F32), pltpu.VMEM((bm, D), BF16), pltpu.VMEM((bm, D), F32),
                            pltpu.VMEM((2, bm * C, LANES), F32), pltpu.SemaphoreType.DMA((2,)),
                            pltpu.SemaphoreType.DMA((2,))],
        ),
        out_shape=jax.ShapeDtypeStruct((out_rows * C, LANES), F32),
        compiler_params=_cparams(("arbitrary", "arbitrary")),
        name="moe_experts",
    )(blk_e, tok_buf, dst, h, wg, wu, wd)


def _moe_combine_kernel(y0_ref, y1_ref, gate_ref, x_ref, mod_ref, ng_ref, o_ref):
    gate = gate_ref[...]
    nc = x_ref.shape[-1] // LANES
    y = _load_row_tiles(y0_ref, nc) * gate[:, 0:1] + _load_row_tiles(y1_ref, nc) * gate[:, 1:2]
    o_ref[...] = x_ref[...] + mod_ref[5:6, :] * _rms(y, ng_ref[3:4, :])


def moe_combine(y, gates, x, mod, ng):
    B, T, D = x.shape
    tm = _tile(T, 512)
    nt = T // tm
    yspec = lambda k: pl.BlockSpec((tm * (D // LANES), LANES), lambda b, i: (k * B * nt + b * nt + i, 0))
    row = lambda n: pl.BlockSpec((None, tm, n), lambda b, i: (b, i, 0))
    return pl.pallas_call(
        _moe_combine_kernel,
        grid=(B, nt),
        in_specs=[yspec(0), yspec(1), row(TOP_K), row(D), pl.BlockSpec((None, 6, D), lambda b, i: (b, 0, 0)),
                  pl.BlockSpec(ng.shape, lambda b, i: (0, 0))],
        out_specs=row(D),
        out_shape=jax.ShapeDtypeStruct((B, T, D), F32),
        compiler_params=_cparams(("parallel", "parallel")),
        name="moe_combine",
    )(y, y, gates, x, mod, ng)


def _routing_tables(top_idx, bm):
    n = top_idx.shape[0]
    flat_e = top_idx.reshape(-1)
    onehot = (flat_e[:, None] == jnp.arange(N_EXPERTS, dtype=jnp.int32)[None, :]).astype(jnp.int32)
    csum = jnp.cumsum(onehot, axis=0)
    rank = jnp.sum(csum * onehot, axis=1) - 1
    counts = csum[-1]
    padded = (counts + bm - 1) // bm * bm
    pad_end = jnp.cumsum(padded)
    pad_start = pad_end - padded
    pos = jnp.sum(pad_start[None, :] * onehot, axis=1) + rank
    rows = n * TOP_K + N_EXPERTS * bm
    flat = jnp.arange(n * TOP_K, dtype=jnp.int32)
    src = jnp.full((rows,), -1, jnp.int32).at[pos].set(flat)
    is_pad = src < 0
    tok_buf = jnp.where(is_pad, 0, src // TOP_K)
    spare = n * TOP_K + jnp.cumsum(is_pad.astype(jnp.int32)) - 1
    dst = jnp.where(is_pad, spare, (src % TOP_K) * n + src // TOP_K)
    dst = jnp.concatenate([rows + jnp.arange(bm, dtype=jnp.int32), dst])
    blk_start = jnp.arange(rows // bm, dtype=jnp.int32) * bm
    blk_e = jnp.minimum(jnp.searchsorted(pad_end, blk_start, side='right'), N_EXPERTS - 1).astype(jnp.int32)
    return tok_buf, dst.astype(jnp.int32), blk_e, rows + bm


def moe(x, mod, ng, w_router, wg, wu, wd, bm):
    B, T, D = x.shape
    h, top_idx, gates = router(x, mod, ng, w_router)
    tok_buf, dst, blk_e, out_rows = _routing_tables(top_idx.reshape(B * T, TOP_K), bm)
    y = moe_experts(h.reshape(-1, LANES), tok_buf, dst, blk_e, wg, wu, wd, bm, out_rows)
    return moe_combine(y, gates, x, mod, ng)


def _prep_weights(p):
    w = {}
    w['l0'] = _prep_l0_weights(p['w_in_ab'][0], p['mla_w_qb'][0], p['mla_w_kvb'][0])
    w['qn'] = p['mla_q_norm'][0][None, :]
    w['kvn'] = p['mla_kv_norm'][0][None, :]
    w_out = p['w_out_ab'][0]
    na = MLA_HEADS * MLA_V
    g = SWA_HEADS // SWA_KV_HEADS
    D = w_out.shape[1]
    wa = jnp.pad(w_out[:na].reshape(MLA_HEADS, MLA_V, D), ((0, 0), (0, LANES - MLA_V), (0, 0)))
    w['wa'] = wa.reshape(MLA_HEADS * LANES, D).astype(BF16)
    w['wb'] = w_out[na:].reshape(SWA_KV_HEADS, g, SWA_HEAD_DIM, D).transpose(1, 0, 2, 3).reshape(-1, D).astype(BF16)
    w['ffn'] = tuple(p[k][0].astype(BF16) for k in ('ffn_w_gate', 'ffn_w_up', 'ffn_w_down'))
    w_in_c = p['w_in_c'][0]
    ncol = w_in_c.shape[1]
    w['win_c'] = jnp.pad(w_in_c, ((0, 0), (0, LANES - 2 * GLA_GATE_RANK))).astype(BF16)
    wgu = p['gla_w_gate_up'][0]
    nk = wgu.shape[-1]
    zero = jnp.zeros((GLA_GATE_RANK, nk), F32)
    wgate = jnp.concatenate([jnp.concatenate([wgu[0], zero], axis=1), jnp.concatenate([zero, wgu[1]], axis=1)], axis=0)
    w['wgate'] = jnp.pad(wgate, ((0, LANES - 2 * GLA_GATE_RANK), (0, 0))).astype(BF16)
    w['bgate'] = p['gla_b_gate_up'][0].reshape(1, 2 * nk)
    w['nk'] = nk
    w['nv'] = (ncol - 2 * nk - 2 * GLA_GATE_RANK) // 2
    w['out_norm'] = p['gla_out_norm'][0][None, :]
    w['w_out_c'] = p['w_out_c'][0].astype(BF16)
    w['router'] = p['moe_router'][0]
    w['moe'] = tuple(p[k][0].astype(BF16) for k in ('moe_w_gate', 'moe_w_up', 'moe_w_down'))
    w['swa_bias'] = swa_bias(p['rel_bias'])
    w['sink'] = p['swa_sink'][0]
    return w


def _trunk(x, c, p, w, moe_bm):
    B, T, D = x.shape
    tables = _rope_tables(T)
    mod = ada_mod(c, p['w_ada'], p['b_ada'], 0)
    ng = p['norm_gain'][0]
    q, k, v, sq, sk, sv = l0_proj(x, mod, ng, w['l0'][0], w['qn'], w['kvn'], *w['l0'][1:], tables)
    a = mla_attention(q, k, v)
    b = swa_attention(sq, sk, sv, w['sink'], w['swa_bias'])
    x = out_res(a, b, w['wa'], w['wb'], x, mod, ng)
    x = ffn(x, mod, ng, *w['ffn'])
    mod = ada_mod(c, p['w_ada'], p['b_ada'], 1)
    ng = p['norm_gain'][1]
    q, k, v, r, gf, gb = l1_proj(x, mod, ng, w['win_c'], w['wgate'], w['bgate'], w['nk'], w['nv'])
    o_f = gla_scan(q, k, gf, v, reverse=False)
    o_b = gla_scan(q, k, gb, v, reverse=True)
    x = gla_out(o_f, o_b, r, w['out_norm'], w['w_out_c'], x, mod, ng)
    return moe(x, mod, ng, w['router'], *w['moe'], moe_bm)


def kernel(x_prompt, x_sample, c_prompt, c_sample, norm_gain, w_ada, b_ada, rel_bias, w_in_ab, mla_q_norm, mla_kv_norm, mla_w_qb, mla_w_kvb, swa_sink, w_out_ab, ffn_w_gate, ffn_w_up, ffn_w_down, w_in_c, gla_w_gate_up, gla_b_gate_up, gla_out_norm, w_out_c, moe_router, moe_w_gate, moe_w_up, moe_w_down):
    p = dict(norm_gain=norm_gain, w_ada=w_ada, b_ada=b_ada, rel_bias=rel_bias, w_in_ab=w_in_ab,
             mla_q_norm=mla_q_norm, mla_kv_norm=mla_kv_norm, mla_w_qb=mla_w_qb, mla_w_kvb=mla_w_kvb,
             swa_sink=swa_sink, w_out_ab=w_out_ab, ffn_w_gate=ffn_w_gate, ffn_w_up=ffn_w_up,
             ffn_w_down=ffn_w_down, w_in_c=w_in_c, gla_w_gate_up=gla_w_gate_up, gla_b_gate_up=gla_b_gate_up,
             gla_out_norm=gla_out_norm, w_out_c=w_out_c, moe_router=moe_router, moe_w_gate=moe_w_gate,
             moe_w_up=moe_w_up, moe_w_down=moe_w_down)
    w = _prep_weights(p)
    bm = 512
    return (_trunk(x_prompt, c_prompt, p, w, bm), _trunk(x_sample, c_sample, p, w, bm))
```

```python
import functools
import math

import jax
import jax.numpy as jnp
from jax import lax
from jax.experimental import pallas as pl
from jax.experimental.pallas import tpu as pltpu

F32 = jnp.float32
BF16 = jnp.bfloat16

MLA_HEADS = 8
MLA_NOPE = 64
MLA_ROPE = 32
MLA_V = 64
MLA_Q_LORA = 384
MLA_KV_LORA = 256
ROPE_THETA = 10000.0
SWA_HEADS = 8
SWA_KV_HEADS = 2
SWA_HEAD_DIM = 64
WINDOW = 128
REL_BUCKETS = 32
REL_MAX_DIST = 128
GLA_HEADS = 4
GLA_GATE_RANK = 16
GLA_GATE_NORM = 16.0
GLA_CHUNK = 64
N_EXPERTS = 8
TOP_K = 2
EPS = 1e-6

LANES = 128
VMEM_LIMIT = 56 * 1024 * 1024


def _cparams(sem, vmem=VMEM_LIMIT):
    return pltpu.CompilerParams(dimension_semantics=sem, vmem_limit_bytes=vmem)


def _tile(n, pref):
    if n <= pref:
        return n
    t = pref
    while n % t:
        t -= 8
    return t


def _rms(x, g):
    return x * lax.rsqrt(jnp.mean(x * x, axis=-1, keepdims=True) + EPS) * g


def _modulate(x, g, shift, scale):
    return _rms(x, g) * (1.0 + scale) + shift


def _silu(x):
    return x / (1.0 + jnp.exp(-x))


def _dot(a, b):
    return jnp.dot(a, b, preferred_element_type=F32)


def _dot_nt(a, b):
    return lax.dot_general(a, b, (((1,), (1,)), ((), ())), preferred_element_type=F32)


def _dot_tn(a, b):
    return lax.dot_general(a, b, (((0,), (0,)), ((), ())), preferred_element_type=F32)


def _ada_kernel(c_ref, w_ref, b_ref, o_ref):
    s = _silu(c_ref[...]).astype(BF16)
    o_ref[...] = _dot(s, w_ref[...].astype(BF16)) + b_ref[...]


def ada_mod(c, w_ada, b_ada, layer):
    B, D = c.shape
    N = w_ada.shape[-1]
    Bp = -(-B // 16) * 16
    cp = jnp.pad(c, ((0, Bp - B), (0, 0)))
    tn = _tile(N, 1536)
    out = pl.pallas_call(
        _ada_kernel,
        grid=(N // tn,),
        in_specs=[pl.BlockSpec((Bp, D), lambda j: (0, 0)),
                  pl.BlockSpec((None, D, tn), lambda j: (layer, 0, j)),
                  pl.BlockSpec((None, 1, tn), lambda j: (layer, 0, j))],
        out_specs=pl.BlockSpec((Bp, tn), lambda j: (0, j)),
        out_shape=jax.ShapeDtypeStruct((Bp, N), F32),
        compiler_params=_cparams(("parallel",)),
        name="ada_mod",
    )(cp, w_ada, b_ada.reshape(b_ada.shape[0], 1, N))
    return out[:B].reshape(B, 6, D)


def _l0_proj_kernel(x_ref, mod_ref, ng_ref, win_ref, qn_ref, kvn_ref, wq_ref, wqs_ref, wk_ref, wv_ref,
                    cq_ref, sq_ref, ck_ref, sk_ref,
                    q_ref, k_ref, v_ref, sq_out_ref, sk_out_ref, sv_out_ref):
    h = _modulate(x_ref[...], ng_ref[0:1, :], mod_ref[0:1, :], mod_ref[1:2, :]).astype(BF16)
    p = _dot(h, win_ref[...])
    o1 = MLA_Q_LORA
    o2 = o1 + MLA_KV_LORA
    nq = _rms(p[:, :o1], qn_ref[...]).astype(BF16)
    q = _dot(nq, wq_ref[...])
    qs = _dot(nq, wqs_ref[...])
    cq = jnp.concatenate([cq_ref[...]] * MLA_HEADS, axis=1)
    sq = jnp.concatenate([sq_ref[...]] * MLA_HEADS, axis=1)
    q_ref[...] = (q * cq + qs * sq).astype(q_ref.dtype)
    nkv = _rms(p[:, o1:o2], kvn_ref[...]).astype(BF16)
    kpe = p[:, o2:o2 + LANES] * ck_ref[...] + p[:, o2 + LANES:o2 + 2 * LANES] * sk_ref[...]
    k = _dot(nkv, wk_ref[...]) + jnp.concatenate([kpe] * MLA_HEADS, axis=1)
    k_ref[...] = k.astype(k_ref.dtype)
    v = _dot(nkv, wv_ref[...])
    lane = lax.broadcasted_iota(jnp.int32, v.shape, 1)
    v_ref[...] = jnp.where((lane & (LANES - 1)) == MLA_V, 1.0, v).astype(v_ref.dtype)
    o3 = o2 + 2 * LANES
    o4 = o3 + SWA_HEADS * LANES
    sq_out_ref[...] = (p[:, o3:o4] * (SWA_HEAD_DIM ** -0.5)).astype(sq_out_ref.dtype)
    sk_out_ref[...] = p[:, o4:o4 + LANES].astype(sk_out_ref.dtype)
    sv_out_ref[...] = p[:, o4 + LANES:o4 + 2 * LANES].astype(sv_out_ref.dtype)


def _prep_l0_weights(w_in, w_qb, w_kvb):
    D = w_in.shape[0]
    o1 = MLA_Q_LORA
    o2 = o1 + MLA_KV_LORA
    o2r = o2 + MLA_ROPE
    half = MLA_ROPE // 2
    padl = lambda w, l, r: jnp.pad(w, ((0, 0), (l, r)))
    kpe = w_in[:, o2:o2r]
    kpe_sw = jnp.concatenate([-kpe[:, half:], kpe[:, :half]], axis=1)
    o3 = o2r + SWA_HEADS * SWA_HEAD_DIM
    o4 = o3 + SWA_KV_HEADS * SWA_HEAD_DIM
    g = SWA_HEADS // SWA_KV_HEADS
    swa_q = []
    for h in range(SWA_HEADS):
        hk = h // g
        blk = w_in[:, o2r + h * SWA_HEAD_DIM: o2r + (h + 1) * SWA_HEAD_DIM]
        swa_q.append(padl(blk, hk * SWA_HEAD_DIM, LANES - (hk + 1) * SWA_HEAD_DIM))
    win = jnp.concatenate(
        [w_in[:, :o2], padl(kpe, MLA_NOPE, LANES - MLA_NOPE - MLA_ROPE),
         padl(kpe_sw, MLA_NOPE, LANES - MLA_NOPE - MLA_ROPE)] + swa_q + [w_in[:, o3:o4], w_in[:, o4:]],
        axis=1).astype(BF16)
    dqk = MLA_NOPE + MLA_ROPE
    wq3 = w_qb.reshape(o1, MLA_HEADS, dqk)
    wq = jnp.pad(wq3, ((0, 0), (0, 0), (0, LANES - dqk))).reshape(o1, MLA_HEADS * LANES).astype(BF16)
    x1 = wq3[..., MLA_NOPE:MLA_NOPE + half]
    x2 = wq3[..., MLA_NOPE + half:]
    z = jnp.zeros_like
    wqs = jnp.concatenate([z(wq3[..., :MLA_NOPE]), -x2, x1, z(wq3[..., :LANES - dqk])], axis=-1)
    wqs = wqs.reshape(o1, MLA_HEADS * LANES).astype(BF16)
    wkv3 = w_kvb.reshape(MLA_KV_LORA, MLA_HEADS, MLA_NOPE + MLA_V)
    wk = jnp.pad(wkv3[..., :MLA_NOPE], ((0, 0), (0, 0), (0, LANES - MLA_NOPE)))
    wk = wk.reshape(MLA_KV_LORA, MLA_HEADS * LANES).astype(BF16)
    wv = jnp.pad(wkv3[..., MLA_NOPE:], ((0, 0), (0, 0), (0, LANES - MLA_V)))
    wv = wv.reshape(MLA_KV_LORA, MLA_HEADS * LANES).astype(BF16)
    return win, wq, wqs, wk, wv


def _rope_tables(T):
    half = MLA_ROPE // 2
    inv = 1.0 / (ROPE_THETA ** (jnp.arange(half, dtype=F32) / half))
    ang = jnp.arange(T, dtype=F32)[:, None] * inv[None, :]
    cos, sin = jnp.cos(ang), jnp.sin(ang)
    s = (MLA_NOPE + MLA_ROPE) ** -0.5 * math.log2(math.e)
    one = jnp.ones((T, MLA_NOPE), F32)
    zl = jnp.zeros((T, MLA_NOPE), F32)
    zr = jnp.zeros((T, LANES - MLA_NOPE - MLA_ROPE), F32)
    cq = jnp.concatenate([one * s, cos * s, cos * s, zr], axis=1)
    sq = jnp.concatenate([zl, sin * s, sin * s, zr], axis=1)
    ck = jnp.concatenate([zl, cos, cos, zr], axis=1)
    sk = jnp.concatenate([zl, sin, sin, zr], axis=1)
    return cq, sq, ck, sk


def l0_proj(x, mod, ng, win, qn, kvn, wq, wqs, wk, wv, tables):
    B, T, D = x.shape
    tm = _tile(T, 512)
    NP = win.shape[1]
    HQ = MLA_HEADS * LANES
    full = lambda a: pl.BlockSpec(a.shape, lambda b, i: (0,) * a.ndim)
    row = lambda n: pl.BlockSpec((None, tm, n), lambda b, i: (b, i, 0))
    tab = pl.BlockSpec((tm, LANES), lambda b, i: (i, 0))
    outs = [(HQ, BF16), (HQ, BF16), (HQ, BF16), (SWA_HEADS * LANES, BF16), (LANES, BF16), (LANES, BF16)]
    return pl.pallas_call(
        _l0_proj_kernel,
        grid=(B, T // tm),
        in_specs=[row(D), pl.BlockSpec((None, 6, D), lambda b, i: (b, 0, 0)), full(ng), full(win), full(qn),
                  full(kvn), full(wq), full(wqs), full(wk), full(wv), tab, tab, tab, tab],
        out_specs=[row(n) for n, _ in outs],
        out_shape=[jax.ShapeDtypeStruct((B, T, n), dt) for n, dt in outs],
        compiler_params=_cparams(("parallel", "parallel")),
        name="l0_proj",
    )(x, mod, ng, win, qn, kvn, wq, wqs, wk, wv, *tables)


def _mla_attn_kernel(q_ref, k_ref, v_ref, o_ref, m_ref, acc_ref, *, tk):
    T = k_ref.shape[0]
    nk = T // tk
    m_ref[...] = jnp.full(m_ref.shape, -jnp.inf, F32)
    acc_ref[...] = jnp.zeros(acc_ref.shape, F32)
    q = q_ref[...]
    per_trip = next(u for u in (4, 2, 1) if nk % u == 0)

    def body(j, carry):
        r0 = [pl.multiple_of((j * per_trip + u) * tk, tk) for u in range(per_trip)]
        s = [_dot_nt(q, k_ref[pl.ds(r, tk), :]) for r in r0]
        m = m_ref[...]
        acc = acc_ref[...]
        for u in range(per_trip):
            m_new = jnp.maximum(m, jnp.max(s[u], axis=-1, keepdims=True))
            p = jnp.exp2(s[u] - jnp.tile(m_new, (1, tk // LANES)))
            acc = jnp.exp2(m - m_new) * acc + _dot(p.astype(BF16), v_ref[pl.ds(r0[u], tk), :])
            m = m_new
        m_ref[...] = m
        acc_ref[...] = acc
        return carry

    lax.fori_loop(0, nk // per_trip, body, 0)
    acc = acc_ref[...]
    o_ref[...] = (acc / acc[:, MLA_V:MLA_V + 1]).astype(o_ref.dtype)


def mla_attention(q, k, v):
    B, T, _ = q.shape
    tq = _tile(T, 512)
    tk = _tile(T, 1024 if T >= 4096 else 512)
    qspec = pl.BlockSpec((None, tq, LANES), lambda b, h, i: (b, i, h))
    kspec = pl.BlockSpec((None, T, LANES), lambda b, h, i: (b, 0, h))
    return pl.pallas_call(
        functools.partial(_mla_attn_kernel, tk=tk),
        grid=(B, MLA_HEADS, T // tq),
        in_specs=[qspec, kspec, kspec],
        out_specs=qspec,
        out_shape=jax.ShapeDtypeStruct((B, T, MLA_HEADS * LANES), BF16),
        scratch_shapes=[pltpu.VMEM((tq, LANES), F32), pltpu.VMEM((tq, LANES), F32)],
        compiler_params=_cparams(("parallel", "parallel", "parallel")),
        name="mla_attention",
    )(q, k, v)


def _t5_bucket(rel):
    nb = REL_BUCKETS // 2
    max_exact = nb // 2
    n = jnp.abs(rel)
    big = max_exact + (jnp.log(jnp.maximum(n, 1).astype(F32) / max_exact)
                       / math.log(REL_MAX_DIST / max_exact) * (nb - max_exact)).astype(jnp.int32)
    big = jnp.minimum(big, nb - 1)
    return jnp.where(rel > 0, nb, 0) + jnp.where(n < max_exact, n, big)


def _swa_bias_kernel(rb_ref, bucket_ref, o_ref):
    h = pl.program_id(0)
    bucket = bucket_ref[...]
    acc = jnp.zeros(bucket.shape, F32)
    for b in range(REL_BUCKETS):
        acc = jnp.where(bucket == b, rb_ref[b * SWA_HEADS + h], acc)
    qi = lax.broadcasted_iota(jnp.int32, bucket.shape, 0)
    kj = lax.broadcasted_iota(jnp.int32, bucket.shape, 1)
    rel = kj - WINDOW - qi
    o_ref[...] = jnp.where(jnp.abs(rel) <= WINDOW, acc, -jnp.inf)


def swa_bias(rel_bias):
    W = WINDOW
    qi = jnp.arange(W, dtype=jnp.int32)[:, None]
    kj = jnp.arange(3 * W, dtype=jnp.int32)[None, :]
    bucket = _t5_bucket(kj - W - qi)
    return pl.pallas_call(
        _swa_bias_kernel,
        grid=(SWA_HEADS,),
        in_specs=[pl.BlockSpec(memory_space=pltpu.SMEM), pl.BlockSpec((W, 3 * W), lambda h: (0, 0))],
        out_specs=pl.BlockSpec((None, W, 3 * W), lambda h: (h, 0, 0)),
        out_shape=jax.ShapeDtypeStruct((SWA_HEADS, W, 3 * W), F32),
        name="swa_bias",
    )(rel_bias.reshape(-1), bucket)


def _swa_kernel(sink_ref, q_ref, kp_ref, kc_ref, kn_ref, vp_ref, vc_ref, vn_ref, bias_ref, o_ref):
    n = pl.program_id(1)
    nb = pl.num_programs(1)
    W = WINDOW
    kband = jnp.concatenate([kp_ref[...], kc_ref[...], kn_ref[...]], axis=0)
    vband = jnp.concatenate([vp_ref[...], vc_ref[...], vn_ref[...]], axis=0)
    col = lax.broadcasted_iota(jnp.int32, (W, 3 * W), 1)
    lo = jnp.where(n == 0, W, 0)
    hi = jnp.where(n == nb - 1, 2 * W, 3 * W)
    inside = (col >= lo) & (col < hi)
    lane = lax.broadcasted_iota(jnp.int32, (W, LANES), 1)
    G = SWA_HEADS // SWA_KV_HEADS
    heads = range(SWA_HEADS)
    s = [_dot_nt(q_ref[:, h * LANES:(h + 1) * LANES], kband) for h in heads]
    s = [jnp.where(inside, s[h] + bias_ref[h], -jnp.inf) for h in heads]
    m = [jnp.maximum(jnp.max(s[h], axis=-1, keepdims=True), sink_ref[h]) for h in heads]
    p = [jnp.exp(s[h] - m[h]) for h in heads]
    den = [jnp.sum(p[h], axis=-1, keepdims=True) + jnp.exp(sink_ref[h] - m[h]) for h in heads]
    o = [_dot(p[h].astype(BF16), vband) for h in heads]
    o = [o[h] / den[h] for h in heads]
    outs = [jnp.where(lane < SWA_HEAD_DIM, o[g], o[G + g]) for g in range(G)]
    o_ref[...] = jnp.concatenate(outs, axis=1).astype(o_ref.dtype)


def swa_attention(q, k, v, sink, bias):
    B, T, _ = q.shape
    W = WINDOW
    nb = T // W
    G = SWA_HEADS // SWA_KV_HEADS
    prev = pl.BlockSpec((None, W, LANES), lambda b, n: (b, jnp.maximum(n - 1, 0), 0))
    cur = pl.BlockSpec((None, W, LANES), lambda b, n: (b, n, 0))
    nxt = pl.BlockSpec((None, W, LANES), lambda b, n: (b, jnp.minimum(n + 1, nb - 1), 0))
    return pl.pallas_call(
        _swa_kernel,
        grid=(B, nb),
        in_specs=[pl.BlockSpec(memory_space=pltpu.SMEM),
                  pl.BlockSpec((None, W, SWA_HEADS * LANES), lambda b, n: (b, n, 0)),
                  prev, cur, nxt, prev, cur, nxt,
                  pl.BlockSpec(bias.shape, lambda b, n: (0, 0, 0))],
        out_specs=pl.BlockSpec((None, W, G * LANES), lambda b, n: (b, n, 0)),
        out_shape=jax.ShapeDtypeStruct((B, T, G * LANES), BF16),
        compiler_params=_cparams(("parallel", "parallel")),
        name="swa_attention",
    )(sink, q, k, k, k, v, v, v, bias)


def _out_res_kernel(a_ref, b_ref, wa_ref, wb_ref, x_ref, mod_ref, ng_ref, o_ref, *, gate_row, gain_row):
    y = _dot(a_ref[...], wa_ref[...]) + _dot(b_ref[...], wb_ref[...])
    o_ref[...] = x_ref[...] + mod_ref[gate_row:gate_row + 1, :] * _rms(y, ng_ref[gain_row:gain_row + 1, :])


def out_res(a, b, wa, wb, x, mod, ng):
    B, T, D = x.shape
    tm = _tile(T, 512)
    full = lambda w: pl.BlockSpec(w.shape, lambda bb, i: (0,) * w.ndim)
    row = lambda n: pl.BlockSpec((None, tm, n), lambda bb, i: (bb, i, 0))
    return pl.pallas_call(
        functools.partial(_out_res_kernel, gate_row=2, gain_row=1),
        grid=(B, T // tm),
        in_specs=[row(a.shape[-1]), row(b.shape[-1]), full(wa), full(wb), row(D),
                  pl.BlockSpec((None, 6, D), lambda bb, i: (bb, 0, 0)), full(ng)],
        out_specs=row(D),
        out_shape=jax.ShapeDtypeStruct((B, T, D), F32),
        compiler_params=_cparams(("parallel", "parallel")),
        name="out_res",
    )(a, b, wa, wb, x, mod, ng)


def _ffn_kernel(x_ref, mod_ref, ng_ref, wg_ref, wu_ref, wd_ref, o_ref):
    x = x_ref[...]
    h = _modulate(x, ng_ref[2:3, :], mod_ref[3:4, :], mod_ref[4:5, :]).astype(BF16)
    a = (_silu(_dot(h, wg_ref[...])) * _dot(h, wu_ref[...])).astype(BF16)
    o_ref[...] = x + mod_ref[5:6, :] * _rms(_dot(a, wd_ref[...]), ng_ref[3:4, :])


def ffn(x, mod, ng, wg, wu, wd):
    B, T, D = x.shape
    tm = _tile(T, 512)
    row = pl.BlockSpec((None, tm, D), lambda b, i: (b, i, 0))
    resident = lambda w: pl.BlockSpec(w.shape, lambda b, i: (0, 0), pipeline_mode=pl.Buffered(1))
    return pl.pallas_call(
        _ffn_kernel,
        grid=(B, T // tm),
        in_specs=[row, pl.BlockSpec((None, 6, D), lambda b, i: (b, 0, 0)),
                  pl.BlockSpec(ng.shape, lambda b, i: (0, 0)), resident(wg), resident(wu), resident(wd)],
        out_specs=row,
        out_shape=jax.ShapeDtypeStruct((B, T, D), F32),
        compiler_params=_cparams(("parallel", "parallel")),
        name="ffn",
    )(x, mod, ng, wg, wu, wd)


def _l1_proj_kernel(x_ref, mod_ref, ng_ref, win_ref, wgate_ref, bgate_ref,
                    q_ref, k_ref, v_ref, r_ref, gf_ref, gb_ref):
    h = _modulate(x_ref[...], ng_ref[0:1, :], mod_ref[0:1, :], mod_ref[1:2, :]).astype(BF16)
    p = _dot(h, win_ref[...])
    nk = q_ref.shape[-1]
    nv = v_ref.shape[-1]
    dk = nk // GLA_HEADS
    q_ref[...] = p[:, :nk] * (dk ** -0.5)
    k_ref[...] = p[:, nk:2 * nk]
    v_ref[...] = p[:, 2 * nk:2 * nk + nv].astype(v_ref.dtype)
    r_ref[...] = p[:, 2 * nk + nv:2 * nk + 2 * nv].astype(r_ref.dtype)
    gd = p[:, 2 * nk + 2 * nv:].astype(BF16)
    pre = _dot(gd, wgate_ref[...]) + bgate_ref[...]
    logsig = jnp.minimum(pre, 0.0) - jnp.log1p(jnp.exp(-jnp.abs(pre)))
    g = logsig / GLA_GATE_NORM
    gf_ref[...] = g[:, :nk]
    gb_ref[...] = g[:, nk:]


def l1_proj(x, mod, ng, win, wgate, bgate, nk, nv):
    B, T, D = x.shape
    tm = _tile(T, 512)
    full = lambda a: pl.BlockSpec(a.shape, lambda b, i: (0,) * a.ndim)
    row = lambda n: pl.BlockSpec((None, tm, n), lambda b, i: (b, i, 0))
    outs = [(nk, F32), (nk, F32), (nv, BF16), (nv, BF16), (nk, F32), (nk, F32)]
    return pl.pallas_call(
        _l1_proj_kernel,
        grid=(B, T // tm),
        in_specs=[row(D), pl.BlockSpec((None, 6, D), lambda b, i: (b, 0, 0)), full(ng), full(win), full(wgate),
                  full(bgate)],
        out_specs=[row(n) for n, _ in outs],
        out_shape=[jax.ShapeDtypeStruct((B, T, n), dt) for n, dt in outs],
        compiler_params=_cparams(("parallel", "parallel")),
        name="l1_proj",
    )(x, mod, ng, win, wgate, bgate)


def _gla_kernel(q_ref, k_ref, g_ref, v_ref, o_ref, st_ref, *, reverse):
    t = pl.program_id(2)
    L = GLA_CHUNK
    nchunk = q_ref.shape[0] // L

    @pl.when(t == 0)
    def _():
        st_ref[...] = jnp.zeros(st_ref.shape, F32)

    ri = lax.broadcasted_iota(jnp.int32, (L, L), 0)
    ci = lax.broadcasted_iota(jnp.int32, (L, L), 1)
    keep = (ci >= ri) if reverse else (ci <= ri)
    order = list(range(nchunk - 1, -1, -1) if reverse else range(nchunk))
    chunks = [slice(c * L, (c + 1) * L) for c in range(nchunk)]

    b = g_ref[...]
    n = b.shape[0]
    pos = lax.broadcasted_iota(jnp.int32, b.shape, 0) & (L - 1)
    step = 1
    while step < L:
        if reverse:
            b = b + jnp.where(pos < L - step, pltpu.roll(b, n - step, axis=0), 0.0)
        else:
            b = b + jnp.where(pos >= step, pltpu.roll(b, step, axis=0), 0.0)
        step *= 2
    ends = [b[c * L:c * L + 1, :] if reverse else b[(c + 1) * L - 1:(c + 1) * L, :] for c in range(nchunk)]
    b_end = jnp.concatenate([jnp.broadcast_to(e, (L, e.shape[1])) for e in ends], axis=0)
    k = k_ref[...]
    qe = (q_ref[...] * jnp.exp(b)).astype(BF16)
    ke = (k * jnp.exp(-b)).astype(BF16)
    kd = (k * jnp.exp(b_end - b)).astype(BF16)
    v = v_ref[...]
    a = [_dot_nt(qe[r], ke[r]) for r in chunks]
    a = [jnp.where(keep, x, 0.0).astype(BF16) for x in a]
    intra = [_dot(a[c], v[chunks[c]]) for c in range(nchunk)]
    upd = [_dot_tn(v[r], kd[r]) for r in chunks]
    st = st_ref[...]
    outs = [None] * nchunk
    for c in order:
        outs[c] = (intra[c] + _dot_nt(qe[chunks[c]], st.astype(BF16))).astype(o_ref.dtype)
        st = st * jnp.exp(ends[c]) + upd[c]
    st_ref[...] = st
    o_ref[...] = jnp.concatenate(outs, axis=0)


def gla_scan(q, k, g, v, reverse):
    B, T, nk = q.shape
    nv = v.shape[-1]
    dk, dv = nk // GLA_HEADS, nv // GLA_HEADS
    tc = _tile(T, 512)
    nt = T // tc
    blk = (lambda t: nt - 1 - t) if reverse else (lambda t: t)
    qspec = pl.BlockSpec((None, tc, dk), lambda b, h, t: (b, blk(t), h))
    vspec = pl.BlockSpec((None, tc, dv), lambda b, h, t: (b, blk(t), h))
    return pl.pallas_call(
        functools.partial(_gla_kernel, reverse=reverse),
        grid=(B, GLA_HEADS, nt),
        in_specs=[qspec, qspec, qspec, vspec],
        out_specs=vspec,
        out_shape=jax.ShapeDtypeStruct((B, T, nv), F32),
        scratch_shapes=[pltpu.VMEM((dv, dk), F32)],
        compiler_params=_cparams(("parallel", "parallel", "arbitrary")),
        name="gla_bwd" if reverse else "gla_fwd",
    )(q, k, g, v)


def _gla_out_kernel(of_ref, ob_ref, r_ref, on_ref, w_ref, x_ref, mod_ref, ng_ref, o_ref):
    o = of_ref[...] + ob_ref[...]
    dv = on_ref.shape[-1]
    parts = [_rms(o[:, h * dv:(h + 1) * dv], on_ref[...]) for h in range(GLA_HEADS)]
    on = jnp.concatenate(parts, axis=1)
    y = _dot((on * _silu(r_ref[...].astype(F32))).astype(BF16), w_ref[...])
    o_ref[...] = x_ref[...] + mod_ref[2:3, :] * _rms(y, ng_ref[1:2, :])


def gla_out(o_f, o_b, r, out_norm, w_out, x, mod, ng):
    B, T, D = x.shape
    tm = _tile(T, 512)
    nv = o_f.shape[-1]
    full = lambda w: pl.BlockSpec(w.shape, lambda bb, i: (0,) * w.ndim)
    row = lambda n: pl.BlockSpec((None, tm, n), lambda bb, i: (bb, i, 0))
    return pl.pallas_call(
        _gla_out_kernel,
        grid=(B, T // tm),
        in_specs=[row(nv), row(nv), row(nv), full(out_norm), full(w_out), row(D),
                  pl.BlockSpec((None, 6, D), lambda bb, i: (bb, 0, 0)), full(ng)],
        out_specs=row(D),
        out_shape=jax.ShapeDtypeStruct((B, T, D), F32),
        compiler_params=_cparams(("parallel", "parallel")),
        name="gla_out",
    )(o_f, o_b, r, out_norm, w_out, x, mod, ng)


def _store_row_tiles(ref, x):
    m = x.shape[0]
    nc = x.shape[1] // LANES
    for c in range(nc):
        ref[pl.ds(c, m, stride=nc), :] = x[:, c * LANES:(c + 1) * LANES]


def _load_row_tiles(ref, nc):
    m = ref.shape[0] // nc
    return jnp.concatenate([ref[pl.ds(c, m, stride=nc), :] for c in range(nc)], axis=1)


def _router_kernel(x_ref, mod_ref, ng_ref, whi_ref, wlo_ref, h_ref, idx_ref, gate_ref):
    h = _modulate(x_ref[...], ng_ref[2:3, :], mod_ref[3:4, :], mod_ref[4:5, :])
    _store_row_tiles(h_ref, h)
    h_hi = h.astype(BF16)
    h_lo = (h - h_hi.astype(F32)).astype(BF16)
    logits = _dot(h_hi, whi_ref[...]) + _dot(h_lo, whi_ref[...]) + _dot(h_hi, wlo_ref[...])
    lane = lax.broadcasted_iota(jnp.int32, logits.shape, 1)
    logits = jnp.where(lane < N_EXPERTS, logits, -jnp.inf)
    m1 = jnp.max(logits, axis=-1, keepdims=True)
    i1 = jnp.min(jnp.where(logits == m1, lane, LANES), axis=-1, keepdims=True)
    rest = jnp.where(lane == i1, -jnp.inf, logits)
    m2 = jnp.max(rest, axis=-1, keepdims=True)
    i2 = jnp.min(jnp.where(rest == m2, lane, LANES), axis=-1, keepdims=True)
    e = jnp.exp(m2 - m1)
    idx_ref[...] = jnp.concatenate([i1, i2], axis=1)
    gate_ref[...] = jnp.concatenate([1.0 / (1.0 + e), e / (1.0 + e)], axis=1)


def router(x, mod, ng, w_router):
    B, T, D = x.shape
    tm = _tile(T, 512)
    wp = jnp.pad(w_router, ((0, 0), (0, LANES - N_EXPERTS)))
    whi = wp.astype(BF16)
    wlo = (wp - whi.astype(F32)).astype(BF16)
    full = lambda w: pl.BlockSpec(w.shape, lambda bb, i: (0,) * w.ndim)
    row = lambda n: pl.BlockSpec((None, tm, n), lambda bb, i: (bb, i, 0))
    return pl.pallas_call(
        _router_kernel,
        grid=(B, T // tm),
        in_specs=[row(D), pl.BlockSpec((None, 6, D), lambda bb, i: (bb, 0, 0)), full(ng), full(whi), full(wlo)],
        out_specs=[pl.BlockSpec((None, tm * (D // LANES), LANES), lambda bb, i: (bb, i, 0)), row(TOP_K), row(TOP_K)],
        out_shape=[jax.ShapeDtypeStruct((B, T * (D // LANES), LANES), F32),
                   jax.ShapeDtypeStruct((B, T, TOP_K), jnp.int32), jax.ShapeDtypeStruct((B, T, TOP_K), F32)],
        compiler_params=_cparams(("parallel", "parallel")),
        name="moe_router",
    )(x, mod, ng, whi, wlo)


def _moe_kernel(blk_e_ref, tok_ref, dst_ref, h_hbm, wg_ref, wu_ref, wd_ref, out_hbm,
                xbuf, ybuf, gsem, ssem, *, bm, nh):
    i = pl.program_id(0)
    nb = pl.num_programs(0)
    slot = lax.rem(i, 2)
    other = 1 - slot
    nc = xbuf.shape[1] // bm

    def tile(row):
        return pl.ds(pl.multiple_of(row * nc, nc), nc)

    def gather(blk, r, s):
        return pltpu.make_async_copy(h_hbm.at[tile(tok_ref[blk * bm + r])], xbuf.at[s, tile(r)], gsem.at[s])

    def scatter(blk, r, s):
        return pltpu.make_async_copy(ybuf.at[s, tile(r)], out_hbm.at[tile(dst_ref[(blk + 1) * bm + r])], ssem.at[s])

    def wait_rows(buf, sem, s):
        pltpu.make_async_copy(h_hbm.at[pl.ds(0, bm * nc)], buf.at[s], sem.at[s]).wait()

    @pl.when(i == 0)
    def _():
        def start(r, c):
            gather(0, r, 0).start()
            return c

        lax.fori_loop(0, bm, start, 0, unroll=8)
        ybuf[1] = jnp.zeros(ybuf.shape[1:], F32)

    wait_rows(xbuf, gsem, slot)
    x = _load_row_tiles(xbuf.at[slot], nc).astype(BF16)

    nxt = jnp.minimum(i + 1, nb - 1)
    for r in range(bm):
        gather(nxt, r, other).start()
        scatter(i - 1, r, other).start()

    tf = wg_ref.shape[1] // nh
    y = None
    for c in range(nh):
        cols = slice(c * tf, (c + 1) * tf)
        a = (_silu(_dot(x, wg_ref[:, cols])) * _dot(x, wu_ref[:, cols])).astype(BF16)
        part = _dot(a, wd_ref[cols, :])
        y = part if y is None else y + part

    @pl.when(i >= 1)
    def _():
        wait_rows(ybuf, ssem, slot)

    _store_row_tiles(ybuf.at[slot], y)

    @pl.when(i == nb - 1)
    def _():
        def start(r, c):
            scatter(i, r, slot).start()
            return c

        lax.fori_loop(0, bm, start, 0, unroll=8)
        wait_rows(ybuf, ssem, slot)
        wait_rows(ybuf, ssem, other)
        wait_rows(xbuf, gsem, other)


def moe_experts(h, tok_buf, dst, blk_e, wg, wu, wd, bm, out_rows):
    D = wg.shape[1]
    C = D // LANES
    rows = tok_buf.shape[0]
    E, _, F = wg.shape
    nh = 2 if F % (4 * LANES) == 0 else 1
    wspec = lambda shape: pl.BlockSpec((None,) + shape, lambda i, be, tk, ds: (be[i], 0, 0),
                                       pipeline_mode=pl.Buffered(1))
    return pl.pallas_call(
        functools.partial(_moe_kernel, bm=bm, nh=nh),
        grid_spec=pltpu.PrefetchScalarGridSpec(
            num_scalar_prefetch=3,
            grid=(rows // bm,),
            in_specs=[pl.BlockSpec(memory_space=pl.ANY), wspec((D, F)), wspec((D, F)), wspec((F, D))],
            out_specs=pl.BlockSpec(memory_space=pl.ANY),
            scratch_shapes=[pltpu.VMEM((2, bm * C, LANES), F32), pltpu.VMEM((2, bm * C, LANES), F32),
                            pltpu.SemaphoreType.DMA((2,)), pltpu.SemaphoreType.DMA((2,))],
        ),
        out_shape=jax.ShapeDtypeStruct((out_rows * C, LANES), F32),
        compiler_params=_cparams(("arbitrary",)),
        name="moe_experts",
    )(blk_e, tok_buf, dst, h, wg, wu, wd)


def _moe_combine_kernel(y0_ref, y1_ref, gate_ref, x_ref, mod_ref, ng_ref, o_ref):
    gate = gate_ref[...]
    nc = x_ref.shape[-1] // LANES
    y = _load_row_tiles(y0_ref, nc) * gate[:, 0:1] + _load_row_tiles(y1_ref, nc) * gate[:, 1:2]
    o_ref[...] = x_ref[...] + mod_ref[5:6, :] * _rms(y, ng_ref[3:4, :])


def moe_combine(y, gates, x, mod, ng):
    B, T, D = x.shape
    tm = _tile(T, 512)
    nt = T // tm
    yspec = lambda k: pl.BlockSpec((tm * (D // LANES), LANES), lambda b, i: (k * B * nt + b * nt + i, 0))
    row = lambda n: pl.BlockSpec((None, tm, n), lambda b, i: (b, i, 0))
    return pl.pallas_call(
        _moe_combine_kernel,
        grid=(B, nt),
        in_specs=[yspec(0), yspec(1), row(TOP_K), row(D), pl.BlockSpec((None, 6, D), lambda b, i: (b, 0, 0)),
                  pl.BlockSpec(ng.shape, lambda b, i: (0, 0))],
        out_specs=row(D),
        out_shape=jax.ShapeDtypeStruct((B, T, D), F32),
        compiler_params=_cparams(("parallel", "parallel")),
        name="moe_combine",
    )(y, y, gates, x, mod, ng)


def _routing_tables(top_idx, bm):
    n = top_idx.shape[0]
    flat_e = top_idx.reshape(-1)
    onehot = (flat_e[:, None] == jnp.arange(N_EXPERTS, dtype=jnp.int32)[None, :]).astype(jnp.int32)
    csum = jnp.cumsum(onehot, axis=0)
    rank = jnp.sum(csum * onehot, axis=1) - 1
    counts = csum[-1]
    padded = (counts + bm - 1) // bm * bm
    pad_end = jnp.cumsum(padded)
    pad_start = pad_end - padded
    pos = jnp.sum(pad_start[None, :] * onehot, axis=1) + rank
    rows = n * TOP_K + N_EXPERTS * bm
    flat = jnp.arange(n * TOP_K, dtype=jnp.int32)
    src = jnp.full((rows,), -1, jnp.int32).at[pos].set(flat, unique_indices=True)
    is_pad = src < 0
    tok_buf = jnp.where(is_pad, 0, src // TOP_K)
    spare = n * TOP_K + jnp.cumsum(is_pad.astype(jnp.int32)) - 1
    dst = jnp.where(is_pad, spare, (src % TOP_K) * n + src // TOP_K)
    dst = jnp.concatenate([rows + jnp.arange(bm, dtype=jnp.int32), dst])
    blk_start = jnp.arange(rows // bm, dtype=jnp.int32) * bm
    blk_e = jnp.minimum(jnp.searchsorted(pad_end, blk_start, side='right'), N_EXPERTS - 1).astype(jnp.int32)
    return tok_buf, dst.astype(jnp.int32), blk_e, rows + bm


def moe(x, mod, ng, w_router, wg, wu, wd, bm):
    B, T, D = x.shape
    h, top_idx, gates = router(x, mod, ng, w_router)
    tok_buf, dst, blk_e, out_rows = _routing_tables(top_idx.reshape(B * T, TOP_K), bm)
    y = moe_experts(h.reshape(-1, LANES), tok_buf, dst, blk_e, wg, wu, wd, bm, out_rows)
    return moe_combine(y, gates, x, mod, ng)


def _prep_weights(p):
    w = {}
    w['l0'] = _prep_l0_weights(p['w_in_ab'][0], p['mla_w_qb'][0], p['mla_w_kvb'][0])
    w['qn'] = p['mla_q_norm'][0][None, :]
    w['kvn'] = p['mla_kv_norm'][0][None, :]
    w_out = p['w_out_ab'][0]
    na = MLA_HEADS * MLA_V
    g = SWA_HEADS // SWA_KV_HEADS
    D = w_out.shape[1]
    wa = jnp.pad(w_out[:na].reshape(MLA_HEADS, MLA_V, D), ((0, 0), (0, LANES - MLA_V), (0, 0)))
    w['wa'] = wa.reshape(MLA_HEADS * LANES, D).astype(BF16)
    w['wb'] = w_out[na:].reshape(SWA_KV_HEADS, g, SWA_HEAD_DIM, D).transpose(1, 0, 2, 3).reshape(-1, D).astype(BF16)
    w['ffn'] = tuple(p[k][0].astype(BF16) for k in ('ffn_w_gate', 'ffn_w_up', 'ffn_w_down'))
    w_in_c = p['w_in_c'][0]
    ncol = w_in_c.shape[1]
    w['win_c'] = jnp.pad(w_in_c, ((0, 0), (0, LANES - 2 * GLA_GATE_RANK))).astype(BF16)
    wgu = p['gla_w_gate_up'][0]
    nk = wgu.shape[-1]
    zero = jnp.zeros((GLA_GATE_RANK, nk), F32)
    wgate = jnp.concatenate([jnp.concatenate([wgu[0], zero], axis=1), jnp.concatenate([zero, wgu[1]], axis=1)], axis=0)
    w['wgate'] = jnp.pad(wgate, ((0, LANES - 2 * GLA_GATE_RANK), (0, 0))).astype(BF16)
    w['bgate'] = p['gla_b_gate_up'][0].reshape(1, 2 * nk)
    w['nk'] = nk
    w['nv'] = (ncol - 2 * nk - 2 * GLA_GATE_RANK) // 2
    w['out_norm'] = p['gla_out_norm'][0][None, :]
    w['w_out_c'] = p['w_out_c'][0].astype(BF16)
    w['router'] = p['moe_router'][0]
    w['moe'] = tuple(p[k][0].astype(BF16) for k in ('moe_w_gate', 'moe_w_up', 'moe_w_down'))
    w['swa_bias'] = swa_bias(p['rel_bias'])
    w['sink'] = p['swa_sink'][0]
    return w


def _trunk(x, c, p, w, moe_bm):
    B, T, D = x.shape
    tables = _rope_tables(T)
    mod = ada_mod(c, p['w_ada'], p['b_ada'], 0)
    ng = p['norm_gain'][0]
    q, k, v, sq, sk, sv = l0_proj(x, mod, ng, w['l0'][0], w['qn'], w['kvn'], *w['l0'][1:], tables)
    a = mla_attention(q, k, v)
    b = swa_attention(sq, sk, sv, w['sink'], w['swa_bias'])
    x = out_res(a, b, w['wa'], w['wb'], x, mod, ng)
    x = ffn(x, mod, ng, *w['ffn'])
    mod = ada_mod(c, p['w_ada'], p['b_ada'], 1)
    ng = p['norm_gain'][1]
    q, k, v, r, gf, gb = l1_proj(x, mod, ng, w['win_c'], w['wgate'], w['bgate'], w['nk'], w['nv'])
    o_f = gla_scan(q, k, gf, v, reverse=False)
    o_b = gla_scan(q, k, gb, v, reverse=True)
    x = gla_out(o_f, o_b, r, w['out_norm'], w['w_out_c'], x, mod, ng)
    return moe(x, mod, ng, w['router'], *w['moe'], moe_bm)


def kernel(x_prompt, x_sample, c_prompt, c_sample, norm_gain, w_ada, b_ada, rel_bias, w_in_ab, mla_q_norm, mla_kv_norm, mla_w_qb, mla_w_kvb, swa_sink, w_out_ab, ffn_w_gate, ffn_w_up, ffn_w_down, w_in_c, gla_w_gate_up, gla_b_gate_up, gla_out_norm, w_out_c, moe_router, moe_w_gate, moe_w_up, moe_w_down):
    p = dict(norm_gain=norm_gain, w_ada=w_ada, b_ada=b_ada, rel_bias=rel_bias, w_in_ab=w_in_ab,
             mla_q_norm=mla_q_norm, mla_kv_norm=mla_kv_norm, mla_w_qb=mla_w_qb, mla_w_kvb=mla_w_kvb,
             swa_sink=swa_sink, w_out_ab=w_out_ab, ffn_w_gate=ffn_w_gate, ffn_w_up=ffn_w_up,
             ffn_w_down=ffn_w_down, w_in_c=w_in_c, gla_w_gate_up=gla_w_gate_up, gla_b_gate_up=gla_b_gate_up,
             gla_out_norm=gla_out_norm, w_out_c=w_out_c, moe_router=moe_router, moe_w_gate=moe_w_gate,
             moe_w_up=moe_w_up, moe_w_down=moe_w_down)
    w = _prep_weights(p)
    bm = 512
    return (_trunk(x_prompt, c_prompt, p, w, bm), _trunk(x_sample, c_sample, p, w, bm))
```

```python
import functools
import math

import jax
import jax.numpy as jnp
from jax import lax
from jax.experimental import pallas as pl
from jax.experimental.pallas import tpu as pltpu

F32 = jnp.float32
BF16 = jnp.bfloat16

MLA_HEADS = 8
MLA_NOPE = 64
MLA_ROPE = 32
MLA_V = 64
MLA_Q_LORA = 384
MLA_KV_LORA = 256
ROPE_THETA = 10000.0
SWA_HEADS = 8
SWA_KV_HEADS = 2
SWA_HEAD_DIM = 64
WINDOW = 128
REL_BUCKETS = 32
REL_MAX_DIST = 128
GLA_HEADS = 4
GLA_GATE_RANK = 16
GLA_GATE_NORM = 16.0
GLA_CHUNK = 64
N_EXPERTS = 8
TOP_K = 2
EPS = 1e-6

LANES = 128
VMEM_LIMIT = 56 * 1024 * 1024


def _cparams(sem, vmem=VMEM_LIMIT):
    return pltpu.CompilerParams(dimension_semantics=sem, vmem_limit_bytes=vmem)


def _tile(n, pref):
    if n <= pref:
        return n
    t = pref
    while n % t:
        t -= 8
    return t


def _rms(x, g):
    return x * lax.rsqrt(jnp.mean(x * x, axis=-1, keepdims=True) + EPS) * g


def _modulate(x, g, shift, scale):
    return _rms(x, g) * (1.0 + scale) + shift


def _silu(x):
    return x / (1.0 + jnp.exp(-x))


def _dot(a, b):
    return jnp.dot(a, b, preferred_element_type=F32)


def _dot_nt(a, b):
    return lax.dot_general(a, b, (((1,), (1,)), ((), ())), preferred_element_type=F32)


def _dot_tn(a, b):
    return lax.dot_general(a, b, (((0,), (0,)), ((), ())), preferred_element_type=F32)


def _ada_kernel(c_ref, w_ref, b_ref, o_ref):
    s = _silu(c_ref[...]).astype(BF16)
    o_ref[...] = _dot(s, w_ref[...].astype(BF16)) + b_ref[...]


def ada_mod(c, w_ada, b_ada, layer):
    B, D = c.shape
    N = w_ada.shape[-1]
    Bp = -(-B // 16) * 16
    cp = jnp.pad(c, ((0, Bp - B), (0, 0)))
    tn = _tile(N, 1536)
    out = pl.pallas_call(
        _ada_kernel,
        grid=(N // tn,),
        in_specs=[pl.BlockSpec((Bp, D), lambda j: (0, 0)),
                  pl.BlockSpec((None, D, tn), lambda j: (layer, 0, j)),
                  pl.BlockSpec((None, 1, tn), lambda j: (layer, 0, j))],
        out_specs=pl.BlockSpec((Bp, tn), lambda j: (0, j)),
        out_shape=jax.ShapeDtypeStruct((Bp, N), F32),
        compiler_params=_cparams(("parallel",)),
        name="ada_mod",
    )(cp, w_ada, b_ada.reshape(b_ada.shape[0], 1, N))
    return out[:B].reshape(B, 6, D)


def _l0_proj_kernel(x_ref, mod_ref, ng_ref, win_ref, qn_ref, kvn_ref, wq_ref, wqs_ref, wk_ref, wv_ref,
                    cq_ref, sq_ref, ck_ref, sk_ref,
                    q_ref, k_ref, v_ref, sq_out_ref, sk_out_ref, sv_out_ref):
    h = _modulate(x_ref[...], ng_ref[0:1, :], mod_ref[0:1, :], mod_ref[1:2, :]).astype(BF16)
    p = _dot(h, win_ref[...])
    o1 = MLA_Q_LORA
    o2 = o1 + MLA_KV_LORA
    nq = _rms(p[:, :o1], qn_ref[...]).astype(BF16)
    q = _dot(nq, wq_ref[...])
    qs = _dot(nq, wqs_ref[...])
    cq = jnp.concatenate([cq_ref[...]] * MLA_HEADS, axis=1)
    sq = jnp.concatenate([sq_ref[...]] * MLA_HEADS, axis=1)
    q_ref[...] = (q * cq + qs * sq).astype(q_ref.dtype)
    nkv = _rms(p[:, o1:o2], kvn_ref[...]).astype(BF16)
    kpe = p[:, o2:o2 + LANES] * ck_ref[...] + p[:, o2 + LANES:o2 + 2 * LANES] * sk_ref[...]
    k = _dot(nkv, wk_ref[...]) + jnp.concatenate([kpe] * MLA_HEADS, axis=1)
    k_ref[...] = k.astype(k_ref.dtype)
    v = _dot(nkv, wv_ref[...])
    lane = lax.broadcasted_iota(jnp.int32, v.shape, 1)
    v_ref[...] = jnp.where((lane & (LANES - 1)) == MLA_V, 1.0, v).astype(v_ref.dtype)
    o3 = o2 + 2 * LANES
    o4 = o3 + SWA_HEADS * LANES
    sq_out_ref[...] = (p[:, o3:o4] * (SWA_HEAD_DIM ** -0.5)).astype(sq_out_ref.dtype)
    sk_out_ref[...] = p[:, o4:o4 + LANES].astype(sk_out_ref.dtype)
    sv_out_ref[...] = p[:, o4 + LANES:o4 + 2 * LANES].astype(sv_out_ref.dtype)


def _prep_l0_weights(w_in, w_qb, w_kvb):
    D = w_in.shape[0]
    o1 = MLA_Q_LORA
    o2 = o1 + MLA_KV_LORA
    o2r = o2 + MLA_ROPE
    half = MLA_ROPE // 2
    padl = lambda w, l, r: jnp.pad(w, ((0, 0), (l, r)))
    kpe = w_in[:, o2:o2r]
    kpe_sw = jnp.concatenate([-kpe[:, half:], kpe[:, :half]], axis=1)
    o3 = o2r + SWA_HEADS * SWA_HEAD_DIM
    o4 = o3 + SWA_KV_HEADS * SWA_HEAD_DIM
    g = SWA_HEADS // SWA_KV_HEADS
    swa_q = []
    for h in range(SWA_HEADS):
        hk = h // g
        blk = w_in[:, o2r + h * SWA_HEAD_DIM: o2r + (h + 1) * SWA_HEAD_DIM]
        swa_q.append(padl(blk, hk * SWA_HEAD_DIM, LANES - (hk + 1) * SWA_HEAD_DIM))
    win = jnp.concatenate(
        [w_in[:, :o2], padl(kpe, MLA_NOPE, LANES - MLA_NOPE - MLA_ROPE),
         padl(kpe_sw, MLA_NOPE, LANES - MLA_NOPE - MLA_ROPE)] + swa_q + [w_in[:, o3:o4], w_in[:, o4:]],
        axis=1).astype(BF16)
    dqk = MLA_NOPE + MLA_ROPE
    wq3 = w_qb.reshape(o1, MLA_HEADS, dqk)
    wq = jnp.pad(wq3, ((0, 0), (0, 0), (0, LANES - dqk))).reshape(o1, MLA_HEADS * LANES).astype(BF16)
    x1 = wq3[..., MLA_NOPE:MLA_NOPE + half]
    x2 = wq3[..., MLA_NOPE + half:]
    z = jnp.zeros_like
    wqs = jnp.concatenate([z(wq3[..., :MLA_NOPE]), -x2, x1, z(wq3[..., :LANES - dqk])], axis=-1)
    wqs = wqs.reshape(o1, MLA_HEADS * LANES).astype(BF16)
    wkv3 = w_kvb.reshape(MLA_KV_LORA, MLA_HEADS, MLA_NOPE + MLA_V)
    wk = jnp.pad(wkv3[..., :MLA_NOPE], ((0, 0), (0, 0), (0, LANES - MLA_NOPE)))
    wk = wk.reshape(MLA_KV_LORA, MLA_HEADS * LANES).astype(BF16)
    wv = jnp.pad(wkv3[..., MLA_NOPE:], ((0, 0), (0, 0), (0, LANES - MLA_V)))
    wv = wv.reshape(MLA_KV_LORA, MLA_HEADS * LANES).astype(BF16)
    return win, wq, wqs, wk, wv


def _rope_tables(T):
    half = MLA_ROPE // 2
    inv = 1.0 / (ROPE_THETA ** (jnp.arange(half, dtype=F32) / half))
    ang = jnp.arange(T, dtype=F32)[:, None] * inv[None, :]
    cos, sin = jnp.cos(ang), jnp.sin(ang)
    s = (MLA_NOPE + MLA_ROPE) ** -0.5 * math.log2(math.e)
    one = jnp.ones((T, MLA_NOPE), F32)
    zl = jnp.zeros((T, MLA_NOPE), F32)
    zr = jnp.zeros((T, LANES - MLA_NOPE - MLA_ROPE), F32)
    cq = jnp.concatenate([one * s, cos * s, cos * s, zr], axis=1)
    sq = jnp.concatenate([zl, sin * s, sin * s, zr], axis=1)
    ck = jnp.concatenate([zl, cos, cos, zr], axis=1)
    sk = jnp.concatenate([zl, sin, sin, zr], axis=1)
    return cq, sq, ck, sk


def l0_proj(x, mod, ng, win, qn, kvn, wq, wqs, wk, wv, tables):
    B, T, D = x.shape
    tm = _tile(T, 512)
    NP = win.shape[1]
    HQ = MLA_HEADS * LANES
    full = lambda a: pl.BlockSpec(a.shape, lambda b, i: (0,) * a.ndim)
    row = lambda n: pl.BlockSpec((None, tm, n), lambda b, i: (b, i, 0))
    tab = pl.BlockSpec((tm, LANES), lambda b, i: (i, 0))
    outs = [(HQ, BF16), (HQ, BF16), (HQ, BF16), (SWA_HEADS * LANES, BF16), (LANES, BF16), (LANES, BF16)]
    return pl.pallas_call(
        _l0_proj_kernel,
        grid=(B, T // tm),
        in_specs=[row(D), pl.BlockSpec((None, 6, D), lambda b, i: (b, 0, 0)), full(ng), full(win), full(qn),
                  full(kvn), full(wq), full(wqs), full(wk), full(wv), tab, tab, tab, tab],
        out_specs=[row(n) for n, _ in outs],
        out_shape=[jax.ShapeDtypeStruct((B, T, n), dt) for n, dt in outs],
        compiler_params=_cparams(("parallel", "parallel")),
        name="l0_proj",
    )(x, mod, ng, win, qn, kvn, wq, wqs, wk, wv, *tables)


def _mla_attn_kernel(q_ref, k_ref, v_ref, o_ref, m_ref, acc_ref, *, tk):
    T = k_ref.shape[0]
    nk = T // tk
    m_ref[...] = jnp.full(m_ref.shape, -jnp.inf, F32)
    acc_ref[...] = jnp.zeros(acc_ref.shape, F32)
    q = q_ref[...]
    per_trip = next(u for u in (8, 4, 2, 1) if nk % u == 0)

    def body(j, carry):
        r0 = [pl.multiple_of((j * per_trip + u) * tk, tk) for u in range(per_trip)]
        s = [_dot_nt(q, k_ref[pl.ds(r, tk), :]) for r in r0]
        m = m_ref[...]
        acc = acc_ref[...]
        for u in range(per_trip):
            m_new = jnp.maximum(m, jnp.max(s[u], axis=-1, keepdims=True))
            p = jnp.exp2(s[u] - jnp.tile(m_new, (1, tk // LANES)))
            acc = jnp.exp2(m - m_new) * acc + _dot(p.astype(BF16), v_ref[pl.ds(r0[u], tk), :])
            m = m_new
        m_ref[...] = m
        acc_ref[...] = acc
        return carry

    lax.fori_loop(0, nk // per_trip, body, 0)
    acc = acc_ref[...]
    o_ref[...] = (acc / acc[:, MLA_V:MLA_V + 1]).astype(o_ref.dtype)


def mla_attention(q, k, v):
    B, T, _ = q.shape
    tq = _tile(T, 512)
    tk = _tile(T, 1024 if T >= 4096 else 512)
    qspec = pl.BlockSpec((None, tq, LANES), lambda b, h, i: (b, i, h))
    kspec = pl.BlockSpec((None, T, LANES), lambda b, h, i: (b, 0, h))
    return pl.pallas_call(
        functools.partial(_mla_attn_kernel, tk=tk),
        grid=(B, MLA_HEADS, T // tq),
        in_specs=[qspec, kspec, kspec],
        out_specs=qspec,
        out_shape=jax.ShapeDtypeStruct((B, T, MLA_HEADS * LANES), BF16),
        scratch_shapes=[pltpu.VMEM((tq, LANES), F32), pltpu.VMEM((tq, LANES), F32)],
        compiler_params=_cparams(("parallel", "parallel", "parallel")),
        name="mla_attention",
    )(q, k, v)


def _t5_bucket(rel):
    nb = REL_BUCKETS // 2
    max_exact = nb // 2
    n = jnp.abs(rel)
    big = max_exact + (jnp.log(jnp.maximum(n, 1).astype(F32) / max_exact)
                       / math.log(REL_MAX_DIST / max_exact) * (nb - max_exact)).astype(jnp.int32)
    big = jnp.minimum(big, nb - 1)
    return jnp.where(rel > 0, nb, 0) + jnp.where(n < max_exact, n, big)


def _swa_bias_kernel(rb_ref, bucket_ref, o_ref):
    h = pl.program_id(0)
    bucket = bucket_ref[...]
    acc = jnp.zeros(bucket.shape, F32)
    for b in range(REL_BUCKETS):
        acc = jnp.where(bucket == b, rb_ref[b * SWA_HEADS + h], acc)
    qi = lax.broadcasted_iota(jnp.int32, bucket.shape, 0)
    kj = lax.broadcasted_iota(jnp.int32, bucket.shape, 1)
    rel = kj - WINDOW - qi
    o_ref[...] = jnp.where(jnp.abs(rel) <= WINDOW, acc, -jnp.inf)


def swa_bias(rel_bias):
    W = WINDOW
    qi = jnp.arange(W, dtype=jnp.int32)[:, None]
    kj = jnp.arange(3 * W, dtype=jnp.int32)[None, :]
    bucket = _t5_bucket(kj - W - qi)
    return pl.pallas_call(
        _swa_bias_kernel,
        grid=(SWA_HEADS,),
        in_specs=[pl.BlockSpec(memory_space=pltpu.SMEM), pl.BlockSpec((W, 3 * W), lambda h: (0, 0))],
        out_specs=pl.BlockSpec((None, W, 3 * W), lambda h: (h, 0, 0)),
        out_shape=jax.ShapeDtypeStruct((SWA_HEADS, W, 3 * W), F32),
        name="swa_bias",
    )(rel_bias.reshape(-1), bucket)


def _swa_kernel(sink_ref, q_ref, kp_ref, kc_ref, kn_ref, vp_ref, vc_ref, vn_ref, bias_ref, o_ref):
    n = pl.program_id(1)
    nb = pl.num_programs(1)
    W = WINDOW
    kband = jnp.concatenate([kp_ref[...], kc_ref[...], kn_ref[...]], axis=0)
    vband = jnp.concatenate([vp_ref[...], vc_ref[...], vn_ref[...]], axis=0)
    col = lax.broadcasted_iota(jnp.int32, (W, 3 * W), 1)
    lo = jnp.where(n == 0, W, 0)
    hi = jnp.where(n == nb - 1, 2 * W, 3 * W)
    inside = (col >= lo) & (col < hi)
    lane = lax.broadcasted_iota(jnp.int32, (W, LANES), 1)
    G = SWA_HEADS // SWA_KV_HEADS
    heads = range(SWA_HEADS)
    s = [_dot_nt(q_ref[:, h * LANES:(h + 1) * LANES], kband) for h in heads]
    s = [jnp.where(inside, s[h] + bias_ref[h], -jnp.inf) for h in heads]
    m = [jnp.maximum(jnp.max(s[h], axis=-1, keepdims=True), sink_ref[h]) for h in heads]
    p = [jnp.exp(s[h] - m[h]) for h in heads]
    den = [jnp.sum(p[h], axis=-1, keepdims=True) + jnp.exp(sink_ref[h] - m[h]) for h in heads]
    o = [_dot(p[h].astype(BF16), vband) for h in heads]
    o = [o[h] / den[h] for h in heads]
    outs = [jnp.where(lane < SWA_HEAD_DIM, o[g], o[G + g]) for g in range(G)]
    o_ref[...] = jnp.concatenate(outs, axis=1).astype(o_ref.dtype)


def swa_attention(q, k, v, sink, bias):
    B, T, _ = q.shape
    W = WINDOW
    nb = T // W
    G = SWA_HEADS // SWA_KV_HEADS
    prev = pl.BlockSpec((None, W, LANES), lambda b, n: (b, jnp.maximum(n - 1, 0), 0))
    cur = pl.BlockSpec((None, W, LANES), lambda b, n: (b, n, 0))
    nxt = pl.BlockSpec((None, W, LANES), lambda b, n: (b, jnp.minimum(n + 1, nb - 1), 0))
    return pl.pallas_call(
        _swa_kernel,
        grid=(B, nb),
        in_specs=[pl.BlockSpec(memory_space=pltpu.SMEM),
                  pl.BlockSpec((None, W, SWA_HEADS * LANES), lambda b, n: (b, n, 0)),
                  prev, cur, nxt, prev, cur, nxt,
                  pl.BlockSpec(bias.shape, lambda b, n: (0, 0, 0))],
        out_specs=pl.BlockSpec((None, W, G * LANES), lambda b, n: (b, n, 0)),
        out_shape=jax.ShapeDtypeStruct((B, T, G * LANES), BF16),
        compiler_params=_cparams(("parallel", "parallel")),
        name="swa_attention",
    )(sink, q, k, k, k, v, v, v, bias)


def _mix_ffn_kernel(a_ref, b_ref, wa_ref, wb_ref, x_ref, mod_ref, ng_ref, wg_ref, wu_ref, wd_ref, o_ref):
    y = _dot(a_ref[...], wa_ref[...]) + _dot(b_ref[...], wb_ref[...])
    x = x_ref[...] + mod_ref[2:3, :] * _rms(y, ng_ref[1:2, :])
    h = _modulate(x, ng_ref[2:3, :], mod_ref[3:4, :], mod_ref[4:5, :]).astype(BF16)
    u = (_silu(_dot(h, wg_ref[...])) * _dot(h, wu_ref[...])).astype(BF16)
    o_ref[...] = x + mod_ref[5:6, :] * _rms(_dot(u, wd_ref[...]), ng_ref[3:4, :])


def mix_ffn(a, b, wa, wb, x, mod, ng, wg, wu, wd):
    B, T, D = x.shape
    tm = _tile(T, 512)
    row = lambda n: pl.BlockSpec((None, tm, n), lambda bb, i: (bb, i, 0))
    resident = lambda w: pl.BlockSpec(w.shape, lambda bb, i: (0, 0), pipeline_mode=pl.Buffered(1))
    return pl.pallas_call(
        _mix_ffn_kernel,
        grid=(B, T // tm),
        in_specs=[row(a.shape[-1]), row(b.shape[-1]), resident(wa), resident(wb), row(D),
                  pl.BlockSpec((None, 6, D), lambda bb, i: (bb, 0, 0)), pl.BlockSpec(ng.shape, lambda bb, i: (0, 0)),
                  resident(wg), resident(wu), resident(wd)],
        out_specs=row(D),
        out_shape=jax.ShapeDtypeStruct((B, T, D), F32),
        compiler_params=_cparams(("parallel", "parallel")),
        name="mix_ffn",
    )(a, b, wa, wb, x, mod, ng, wg, wu, wd)


def _l1_proj_kernel(x_ref, mod_ref, ng_ref, win_ref, wgate_ref, bgate_ref,
                    q_ref, k_ref, v_ref, r_ref, gf_ref, gb_ref):
    h = _modulate(x_ref[...], ng_ref[0:1, :], mod_ref[0:1, :], mod_ref[1:2, :]).astype(BF16)
    nk = q_ref.shape[-1]
    nv = v_ref.shape[-1]
    dk = nk // GLA_HEADS
    ng_cols = 2 * nk + 2 * nv
    gd = _dot(h, win_ref[:, ng_cols:]).astype(BF16)
    pre = _dot(gd, wgate_ref[...]) + bgate_ref[...]
    p = _dot(h, win_ref[:, :ng_cols])
    logsig = jnp.minimum(pre, 0.0) - jnp.log1p(jnp.exp(-jnp.abs(pre)))
    g = logsig / GLA_GATE_NORM
    gf_ref[...] = g[:, :nk]
    gb_ref[...] = g[:, nk:]
    q_ref[...] = p[:, :nk] * (dk ** -0.5)
    k_ref[...] = p[:, nk:2 * nk]
    v_ref[...] = p[:, 2 * nk:2 * nk + nv].astype(v_ref.dtype)
    r_ref[...] = p[:, 2 * nk + nv:].astype(r_ref.dtype)


def l1_proj(x, mod, ng, win, wgate, bgate, nk, nv):
    B, T, D = x.shape
    tm = _tile(T, 512)
    full = lambda a: pl.BlockSpec(a.shape, lambda b, i: (0,) * a.ndim)
    row = lambda n: pl.BlockSpec((None, tm, n), lambda b, i: (b, i, 0))
    outs = [(nk, F32), (nk, F32), (nv, BF16), (nv, BF16), (nk, F32), (nk, F32)]
    return pl.pallas_call(
        _l1_proj_kernel,
        grid=(B, T // tm),
        in_specs=[row(D), pl.BlockSpec((None, 6, D), lambda b, i: (b, 0, 0)), full(ng), full(win), full(wgate),
                  full(bgate)],
        out_specs=[row(n) for n, _ in outs],
        out_shape=[jax.ShapeDtypeStruct((B, T, n), dt) for n, dt in outs],
        compiler_params=_cparams(("parallel", "parallel")),
        name="l1_proj",
    )(x, mod, ng, win, wgate, bgate)


def _gla_kernel(q_ref, k_ref, g_ref, v_ref, o_ref, st_ref, *, reverse):
    t = pl.program_id(2)
    L = GLA_CHUNK
    nchunk = q_ref.shape[0] // L

    @pl.when(t == 0)
    def _():
        st_ref[...] = jnp.zeros(st_ref.shape, F32)

    ri = lax.broadcasted_iota(jnp.int32, (L, L), 0)
    ci = lax.broadcasted_iota(jnp.int32, (L, L), 1)
    keep = (ci >= ri) if reverse else (ci <= ri)
    order = list(range(nchunk - 1, -1, -1) if reverse else range(nchunk))
    chunks = [slice(c * L, (c + 1) * L) for c in range(nchunk)]

    b = g_ref[...]
    n = b.shape[0]
    pos = lax.broadcasted_iota(jnp.int32, b.shape, 0) & (L - 1)
    step = 1
    while step < L:
        if reverse:
            b = b + jnp.where(pos < L - step, pltpu.roll(b, n - step, axis=0), 0.0)
        else:
            b = b + jnp.where(pos >= step, pltpu.roll(b, step, axis=0), 0.0)
        step *= 2
    ends = [b[c * L:c * L + 1, :] if reverse else b[(c + 1) * L - 1:(c + 1) * L, :] for c in range(nchunk)]
    b_end = jnp.concatenate([jnp.broadcast_to(e, (L, e.shape[1])) for e in ends], axis=0)
    k = k_ref[...]
    qe = (q_ref[...] * jnp.exp(b)).astype(BF16)
    ke = (k * jnp.exp(-b)).astype(BF16)
    kd = (k * jnp.exp(b_end - b)).astype(BF16)
    v = v_ref[...]
    a = [_dot_nt(qe[r], ke[r]) for r in chunks]
    a = [jnp.where(keep, x, 0.0).astype(BF16) for x in a]
    intra = [_dot(a[c], v[chunks[c]]) for c in range(nchunk)]
    upd = [_dot_tn(v[r], kd[r]) for r in chunks]
    st = st_ref[...]
    outs = [None] * nchunk
    for c in order:
        outs[c] = (intra[c] + _dot_nt(qe[chunks[c]], st.astype(BF16))).astype(o_ref.dtype)
        st = st * jnp.exp(ends[c]) + upd[c]
    st_ref[...] = st
    o_ref[...] = jnp.concatenate(outs, axis=0)


def gla_scan(q, k, g, v, reverse):
    B, T, nk = q.shape
    nv = v.shape[-1]
    dk, dv = nk // GLA_HEADS, nv // GLA_HEADS
    tc = _tile(T, 1024)
    nt = T // tc
    blk = (lambda t: nt - 1 - t) if reverse else (lambda t: t)
    qspec = pl.BlockSpec((None, tc, dk), lambda b, h, t: (b, blk(t), h))
    vspec = pl.BlockSpec((None, tc, dv), lambda b, h, t: (b, blk(t), h))
    return pl.pallas_call(
        functools.partial(_gla_kernel, reverse=reverse),
        grid=(B, GLA_HEADS, nt),
        in_specs=[qspec, qspec, qspec, vspec],
        out_specs=vspec,
        out_shape=jax.ShapeDtypeStruct((B, T, nv), F32),
        scratch_shapes=[pltpu.VMEM((dv, dk), F32)],
        compiler_params=_cparams(("parallel", "parallel", "arbitrary")),
        name="gla_bwd" if reverse else "gla_fwd",
    )(q, k, g, v)


def _store_row_tiles(ref, x):
    m = x.shape[0]
    nc = x.shape[1] // LANES
    for c in range(nc):
        ref[pl.ds(c, m, stride=nc), :] = x[:, c * LANES:(c + 1) * LANES]


def _load_row_tiles(ref, nc):
    m = ref.shape[0] // nc
    return jnp.concatenate([ref[pl.ds(c, m, stride=nc), :] for c in range(nc)], axis=1)


def _gla_router_kernel(of_ref, ob_ref, r_ref, on_ref, w_ref, x_ref, mod_ref, ng_ref, whi_ref, wlo_ref,
                       x1_ref, h_ref, idx_ref, gate_ref):
    o = of_ref[...] + ob_ref[...]
    dv = on_ref.shape[-1]
    on = jnp.concatenate([_rms(o[:, hd * dv:(hd + 1) * dv], on_ref[...]) for hd in range(GLA_HEADS)], axis=1)
    y = _dot((on * _silu(r_ref[...].astype(F32))).astype(BF16), w_ref[...])
    x = x_ref[...] + mod_ref[2:3, :] * _rms(y, ng_ref[1:2, :])
    x1_ref[...] = x
    h = _modulate(x, ng_ref[2:3, :], mod_ref[3:4, :], mod_ref[4:5, :])
    _store_row_tiles(h_ref, h)
    h_hi = h.astype(BF16)
    h_lo = (h - h_hi.astype(F32)).astype(BF16)
    logits = _dot(h_hi, whi_ref[...]) + _dot(h_lo, whi_ref[...]) + _dot(h_hi, wlo_ref[...])
    lane = lax.broadcasted_iota(jnp.int32, logits.shape, 1)
    logits = jnp.where(lane < N_EXPERTS, logits, -jnp.inf)
    m1 = jnp.max(logits, axis=-1, keepdims=True)
    i1 = jnp.min(jnp.where(logits == m1, lane, LANES), axis=-1, keepdims=True)
    rest = jnp.where(lane == i1, -jnp.inf, logits)
    m2 = jnp.max(rest, axis=-1, keepdims=True)
    i2 = jnp.min(jnp.where(rest == m2, lane, LANES), axis=-1, keepdims=True)
    e = jnp.exp(m2 - m1)
    idx_ref[...] = jnp.concatenate([i1, i2], axis=1)
    gate_ref[...] = jnp.concatenate([1.0 / (1.0 + e), e / (1.0 + e)], axis=1)


def gla_router(o_f, o_b, r, out_norm, w_out, x, mod, ng, w_router):
    B, T, D = x.shape
    tm = _tile(T, 512)
    nv = o_f.shape[-1]
    wp = jnp.pad(w_router, ((0, 0), (0, LANES - N_EXPERTS)))
    whi = wp.astype(BF16)
    wlo = (wp - whi.astype(F32)).astype(BF16)
    full = lambda w: pl.BlockSpec(w.shape, lambda bb, i: (0,) * w.ndim)
    row = lambda n: pl.BlockSpec((None, tm, n), lambda bb, i: (bb, i, 0))
    return pl.pallas_call(
        _gla_router_kernel,
        grid=(B, T // tm),
        in_specs=[row(nv), row(nv), row(nv), full(out_norm), full(w_out), row(D),
                  pl.BlockSpec((None, 6, D), lambda bb, i: (bb, 0, 0)), full(ng), full(whi), full(wlo)],
        out_specs=[row(D), pl.BlockSpec((None, tm * (D // LANES), LANES), lambda bb, i: (bb, i, 0)), row(TOP_K),
                   row(TOP_K)],
        out_shape=[jax.ShapeDtypeStruct((B, T, D), F32), jax.ShapeDtypeStruct((B, T * (D // LANES), LANES), F32),
                   jax.ShapeDtypeStruct((B, T, TOP_K), jnp.int32), jax.ShapeDtypeStruct((B, T, TOP_K), F32)],
        compiler_params=_cparams(("parallel", "parallel")),
        name="gla_router",
    )(o_f, o_b, r, out_norm, w_out, x, mod, ng, whi, wlo)


def _moe_kernel(blk_e_ref, tok_ref, dst_ref, h_hbm, wg_ref, wu_ref, wd_ref, out_hbm,
                xbuf, ybuf, gsem, ssem, *, bm, nh):
    i = pl.program_id(0)
    nb = pl.num_programs(0)
    slot = lax.rem(i, 2)
    other = 1 - slot
    nc = xbuf.shape[1] // bm

    def tile(row):
        return pl.ds(pl.multiple_of(row * nc, nc), nc)

    def gather(blk, r, s):
        return pltpu.make_async_copy(h_hbm.at[tile(tok_ref[blk * bm + r])], xbuf.at[s, tile(r)], gsem.at[s])

    def scatter(blk, r, s):
        return pltpu.make_async_copy(ybuf.at[s, tile(r)], out_hbm.at[tile(dst_ref[(blk + 1) * bm + r])], ssem.at[s])

    def wait_rows(buf, sem, s):
        pltpu.make_async_copy(h_hbm.at[pl.ds(0, bm * nc)], buf.at[s], sem.at[s]).wait()

    @pl.when(i == 0)
    def _():
        def start(r, c):
            gather(0, r, 0).start()
            return c

        lax.fori_loop(0, bm, start, 0, unroll=8)
        ybuf[1] = jnp.zeros(ybuf.shape[1:], F32)

    wait_rows(xbuf, gsem, slot)
    x = _load_row_tiles(xbuf.at[slot], nc).astype(BF16)

    nxt = jnp.minimum(i + 1, nb - 1)
    for r in range(bm):
        gather(nxt, r, other).start()
        scatter(i - 1, r, other).start()

    tf = wg_ref.shape[1] // nh
    y = None
    for c in range(nh):
        cols = slice(c * tf, (c + 1) * tf)
        a = (_silu(_dot(x, wg_ref[:, cols])) * _dot(x, wu_ref[:, cols])).astype(BF16)
        part = _dot(a, wd_ref[cols, :])
        y = part if y is None else y + part

    @pl.when(i >= 1)
    def _():
        wait_rows(ybuf, ssem, slot)

    _store_row_tiles(ybuf.at[slot], y)

    @pl.when(i == nb - 1)
    def _():
        def start(r, c):
            scatter(i, r, slot).start()
            return c

        lax.fori_loop(0, bm, start, 0, unroll=8)
        wait_rows(ybuf, ssem, slot)
        wait_rows(ybuf, ssem, other)
        wait_rows(xbuf, gsem, other)


def moe_experts(h, tok_buf, dst, blk_e, wg, wu, wd, bm, out_rows):
    D = wg.shape[1]
    C = D // LANES
    rows = tok_buf.shape[0]
    E, _, F = wg.shape
    nh = 2 if F % (4 * LANES) == 0 else 1
    wspec = lambda shape: pl.BlockSpec((None,) + shape, lambda i, be, tk, ds: (be[i], 0, 0),
                                       pipeline_mode=pl.Buffered(1))
    return pl.pallas_call(
        functools.partial(_moe_kernel, bm=bm, nh=nh),
        grid_spec=pltpu.PrefetchScalarGridSpec(
            num_scalar_prefetch=3,
            grid=(rows // bm,),
            in_specs=[pl.BlockSpec(memory_space=pl.ANY), wspec((D, F)), wspec((D, F)), wspec((F, D))],
            out_specs=pl.BlockSpec(memory_space=pl.ANY),
            scratch_shapes=[pltpu.VMEM((2, bm * C, LANES), F32), pltpu.VMEM((2, bm * C, LANES), F32),
                            pltpu.SemaphoreType.DMA((2,)), pltpu.SemaphoreType.DMA((2,))],
        ),
        out_shape=jax.ShapeDtypeStruct((out_rows * C, LANES), F32),
        compiler_params=_cparams(("arbitrary",)),
        name="moe_experts",
    )(blk_e, tok_buf, dst, h, wg, wu, wd)


def _moe_combine_kernel(y0_ref, y1_ref, gate_ref, x_ref, mod_ref, ng_ref, o_ref):
    gate = gate_ref[...]
    nc = x_ref.shape[-1] // LANES
    y = _load_row_tiles(y0_ref, nc) * gate[:, 0:1] + _load_row_tiles(y1_ref, nc) * gate[:, 1:2]
    o_ref[...] = x_ref[...] + mod_ref[5:6, :] * _rms(y, ng_ref[3:4, :])


def moe_combine(y, gates, x, mod, ng):
    B, T, D = x.shape
    tm = _tile(T, 512)
    nt = T // tm
    yspec = lambda k: pl.BlockSpec((tm * (D // LANES), LANES), lambda b, i: (k * B * nt + b * nt + i, 0))
    row = lambda n: pl.BlockSpec((None, tm, n), lambda b, i: (b, i, 0))
    return pl.pallas_call(
        _moe_combine_kernel,
        grid=(B, nt),
        in_specs=[yspec(0), yspec(1), row(TOP_K), row(D), pl.BlockSpec((None, 6, D), lambda b, i: (b, 0, 0)),
                  pl.BlockSpec(ng.shape, lambda b, i: (0, 0))],
        out_specs=row(D),
        out_shape=jax.ShapeDtypeStruct((B, T, D), F32),
        compiler_params=_cparams(("parallel", "parallel")),
        name="moe_combine",
    )(y, y, gates, x, mod, ng)


def _routing_tables(top_idx, bm):
    n = top_idx.shape[0]
    flat_e = top_idx.reshape(-1)
    onehot = (flat_e[:, None] == jnp.arange(N_EXPERTS, dtype=jnp.int32)[None, :]).astype(jnp.int32)
    csum = jnp.cumsum(onehot, axis=0)
    rank = jnp.sum(csum * onehot, axis=1) - 1
    counts = csum[-1]
    padded = (counts + bm - 1) // bm * bm
    pad_end = jnp.cumsum(padded)
    pad_start = pad_end - padded
    pos = jnp.sum(pad_start[None, :] * onehot, axis=1) + rank
    rows = n * TOP_K + N_EXPERTS * bm
    flat = jnp.arange(n * TOP_K, dtype=jnp.int32)
    src = jnp.full((rows,), -1, jnp.int32).at[pos].set(flat, unique_indices=True)
    is_pad = src < 0
    tok_buf = jnp.where(is_pad, 0, src // TOP_K)
    spare = n * TOP_K + jnp.cumsum(is_pad.astype(jnp.int32)) - 1
    dst = jnp.where(is_pad, spare, (src % TOP_K) * n + src // TOP_K)
    dst = jnp.concatenate([rows + jnp.arange(bm, dtype=jnp.int32), dst])
    blk_start = jnp.arange(rows // bm, dtype=jnp.int32) * bm
    blk_e = jnp.minimum(jnp.searchsorted(pad_end, blk_start, side='right'), N_EXPERTS - 1).astype(jnp.int32)
    return tok_buf, dst.astype(jnp.int32), blk_e, rows + bm


def gla_out_moe(o_f, o_b, r, out_norm, w_out, x, mod, ng, w_router, wg, wu, wd, bm):
    B, T, D = x.shape
    x, h, top_idx, gates = gla_router(o_f, o_b, r, out_norm, w_out, x, mod, ng, w_router)
    tok_buf, dst, blk_e, out_rows = _routing_tables(top_idx.reshape(B * T, TOP_K), bm)
    y = moe_experts(h.reshape(-1, LANES), tok_buf, dst, blk_e, wg, wu, wd, bm, out_rows)
    return moe_combine(y, gates, x, mod, ng)


def _prep_weights(p):
    w = {}
    w['l0'] = _prep_l0_weights(p['w_in_ab'][0], p['mla_w_qb'][0], p['mla_w_kvb'][0])
    w['qn'] = p['mla_q_norm'][0][None, :]
    w['kvn'] = p['mla_kv_norm'][0][None, :]
    w_out = p['w_out_ab'][0]
    na = MLA_HEADS * MLA_V
    g = SWA_HEADS // SWA_KV_HEADS
    D = w_out.shape[1]
    wa = jnp.pad(w_out[:na].reshape(MLA_HEADS, MLA_V, D), ((0, 0), (0, LANES - MLA_V), (0, 0)))
    w['wa'] = wa.reshape(MLA_HEADS * LANES, D).astype(BF16)
    w['wb'] = w_out[na:].reshape(SWA_KV_HEADS, g, SWA_HEAD_DIM, D).transpose(1, 0, 2, 3).reshape(-1, D).astype(BF16)
    w['ffn'] = tuple(p[k][0].astype(BF16) for k in ('ffn_w_gate', 'ffn_w_up', 'ffn_w_down'))
    w_in_c = p['w_in_c'][0]
    ncol = w_in_c.shape[1]
    w['win_c'] = jnp.pad(w_in_c, ((0, 0), (0, LANES - 2 * GLA_GATE_RANK))).astype(BF16)
    wgu = p['gla_w_gate_up'][0]
    nk = wgu.shape[-1]
    zero = jnp.zeros((GLA_GATE_RANK, nk), F32)
    wgate = jnp.concatenate([jnp.concatenate([wgu[0], zero], axis=1), jnp.concatenate([zero, wgu[1]], axis=1)], axis=0)
    w['wgate'] = jnp.pad(wgate, ((0, LANES - 2 * GLA_GATE_RANK), (0, 0))).astype(BF16)
    w['bgate'] = p['gla_b_gate_up'][0].reshape(1, 2 * nk)
    w['nk'] = nk
    w['nv'] = (ncol - 2 * nk - 2 * GLA_GATE_RANK) // 2
    w['out_norm'] = p['gla_out_norm'][0][None, :]
    w['w_out_c'] = p['w_out_c'][0].astype(BF16)
    w['router'] = p['moe_router'][0]
    w['moe'] = tuple(p[k][0].astype(BF16) for k in ('moe_w_gate', 'moe_w_up', 'moe_w_down'))
    w['swa_bias'] = swa_bias(p['rel_bias'])
    w['sink'] = p['swa_sink'][0]
    return w


def _trunk(x, c, p, w, moe_bm):
    B, T, D = x.shape
    tables = _rope_tables(T)
    mod = ada_mod(c, p['w_ada'], p['b_ada'], 0)
    ng = p['norm_gain'][0]
    q, k, v, sq, sk, sv = l0_proj(x, mod, ng, w['l0'][0], w['qn'], w['kvn'], *w['l0'][1:], tables)
    a = mla_attention(q, k, v)
    b = swa_attention(sq, sk, sv, w['sink'], w['swa_bias'])
    x = mix_ffn(a, b, w['wa'], w['wb'], x, mod, ng, *w['ffn'])
    mod = ada_mod(c, p['w_ada'], p['b_ada'], 1)
    ng = p['norm_gain'][1]
    q, k, v, r, gf, gb = l1_proj(x, mod, ng, w['win_c'], w['wgate'], w['bgate'], w['nk'], w['nv'])
    o_f = gla_scan(q, k, gf, v, reverse=False)
    o_b = gla_scan(q, k, gb, v, reverse=True)
    return gla_out_moe(o_f, o_b, r, w['out_norm'], w['w_out_c'], x, mod, ng, w['router'], *w['moe'], moe_bm)


def kernel(x_prompt, x_sample, c_prompt, c_sample, norm_gain, w_ada, b_ada, rel_bias, w_in_ab, mla_q_norm, mla_kv_norm, mla_w_qb, mla_w_kvb, swa_sink, w_out_ab, ffn_w_gate, ffn_w_up, ffn_w_down, w_in_c, gla_w_gate_up, gla_b_gate_up, gla_out_norm, w_out_c, moe_router, moe_w_gate, moe_w_up, moe_w_down):
    p = dict(norm_gain=norm_gain, w_ada=w_ada, b_ada=b_ada, rel_bias=rel_bias, w_in_ab=w_in_ab,
             mla_q_norm=mla_q_norm, mla_kv_norm=mla_kv_norm, mla_w_qb=mla_w_qb, mla_w_kvb=mla_w_kvb,
             swa_sink=swa_sink, w_out_ab=w_out_ab, ffn_w_gate=ffn_w_gate, ffn_w_up=ffn_w_up,
             ffn_w_down=ffn_w_down, w_in_c=w_in_c, gla_w_gate_up=gla_w_gate_up, gla_b_gate_up=gla_b_gate_up,
             gla_out_norm=gla_out_norm, w_out_c=w_out_c, moe_router=moe_router, moe_w_gate=moe_w_gate,
             moe_w_up=moe_w_up, moe_w_down=moe_w_down)
    w = _prep_weights(p)
    bm = 512
    return (_trunk(x_prompt, c_prompt, p, w, bm), _trunk(x_sample, c_sample, p, w, bm))
```

```python
import functools
import math

import jax
import jax.numpy as jnp
from jax import lax
from jax.experimental import pallas as pl
from jax.experimental.pallas import tpu as pltpu

F32 = jnp.float32
BF16 = jnp.bfloat16

MLA_HEADS = 8
MLA_NOPE = 64
MLA_ROPE = 32
MLA_V = 64
MLA_Q_LORA = 384
MLA_KV_LORA = 256
ROPE_THETA = 10000.0
SWA_HEADS = 8
SWA_KV_HEADS = 2
SWA_HEAD_DIM = 64
WINDOW = 128
REL_BUCKETS = 32
REL_MAX_DIST = 128
GLA_HEADS = 4
GLA_GATE_RANK = 16
GLA_GATE_NORM = 16.0
GLA_CHUNK = 64
N_EXPERTS = 8
TOP_K = 2
EPS = 1e-6

LANES = 128
BF16_SUBLANES = 16
VMEM_LIMIT = 56 * 1024 * 1024
SCORE_BYTES = 16 * 1024 * 1024


def _cparams(sem, vmem=VMEM_LIMIT):
    return pltpu.CompilerParams(dimension_semantics=sem, vmem_limit_bytes=vmem)


def _tile(n, pref):
    if n <= pref:
        return n
    t = pref
    while n % t:
        t -= 8
    return t


def _rms(x, g):
    return x * lax.rsqrt(jnp.mean(x * x, axis=-1, keepdims=True) + EPS) * g


def _modulate(x, g, shift, scale):
    return _rms(x, g) * (1.0 + scale) + shift


def _silu(x):
    return x / (1.0 + jnp.exp(-x))


def _dot(a, b):
    return jnp.dot(a, b, preferred_element_type=F32)


def _dot_nt(a, b):
    return lax.dot_general(a, b, (((1,), (1,)), ((), ())), preferred_element_type=F32)


def _dot_tn(a, b):
    return lax.dot_general(a, b, (((0,), (0,)), ((), ())), preferred_element_type=F32)


def _ada_kernel(c_ref, w_ref, b_ref, o_ref):
    s = _silu(c_ref[...]).astype(BF16)
    o_ref[...] = _dot(s, w_ref[...].astype(BF16)) + b_ref[...]


def ada_mod(c, w_ada, b_ada, layer):
    B, D = c.shape
    N = w_ada.shape[-1]
    Bp = -(-B // 16) * 16
    cp = jnp.pad(c, ((0, Bp - B), (0, 0)))
    tn = _tile(N, 1536)
    out = pl.pallas_call(
        _ada_kernel,
        grid=(N // tn,),
        in_specs=[pl.BlockSpec((Bp, D), lambda j: (0, 0)),
                  pl.BlockSpec((None, D, tn), lambda j: (layer, 0, j)),
                  pl.BlockSpec((None, 1, tn), lambda j: (layer, 0, j))],
        out_specs=pl.BlockSpec((Bp, tn), lambda j: (0, j)),
        out_shape=jax.ShapeDtypeStruct((Bp, N), F32),
        compiler_params=_cparams(("parallel",)),
        name="ada_mod",
    )(cp, w_ada, b_ada.reshape(b_ada.shape[0], 1, N))
    return out[:B].reshape(B, 6, D)


def _l0_proj_kernel(x_ref, mod_ref, ng_ref, win_ref, qn_ref, kvn_ref, wq_ref, wqs_ref, wk_ref, wv_ref,
                    cq_ref, sq_ref, ck_ref, sk_ref,
                    q_ref, k_ref, v_ref, sq_out_ref, sk_out_ref, sv_out_ref):
    h = _modulate(x_ref[...], ng_ref[0:1, :], mod_ref[0:1, :], mod_ref[1:2, :]).astype(BF16)
    p = _dot(h, win_ref[...])
    o1 = MLA_Q_LORA
    o2 = o1 + MLA_KV_LORA
    nq = _rms(p[:, :o1], qn_ref[...]).astype(BF16)
    q = _dot(nq, wq_ref[...])
    qs = _dot(nq, wqs_ref[...])
    cq = jnp.concatenate([cq_ref[...]] * MLA_HEADS, axis=1)
    sq = jnp.concatenate([sq_ref[...]] * MLA_HEADS, axis=1)
    q_ref[...] = (q * cq + qs * sq).astype(q_ref.dtype)
    nkv = _rms(p[:, o1:o2], kvn_ref[...]).astype(BF16)
    kpe = p[:, o2:o2 + LANES] * ck_ref[...] + p[:, o2 + LANES:o2 + 2 * LANES] * sk_ref[...]
    k = _dot(nkv, wk_ref[...]) + jnp.concatenate([kpe] * MLA_HEADS, axis=1)
    k_ref[...] = k.astype(k_ref.dtype)
    v = _dot(nkv, wv_ref[...])
    lane = lax.broadcasted_iota(jnp.int32, v.shape, 1)
    v_ref[...] = jnp.where((lane & (LANES - 1)) == MLA_V, 1.0, v).astype(v_ref.dtype)
    o3 = o2 + 2 * LANES
    o4 = o3 + SWA_HEADS * LANES
    sq_out_ref[...] = (p[:, o3:o4] * (SWA_HEAD_DIM ** -0.5)).astype(sq_out_ref.dtype)
    sk_out_ref[...] = p[:, o4:o4 + LANES].astype(sk_out_ref.dtype)
    sv_out_ref[...] = p[:, o4 + LANES:o4 + 2 * LANES].astype(sv_out_ref.dtype)


def _prep_l0_weights(w_in, w_qb, w_kvb):
    D = w_in.shape[0]
    o1 = MLA_Q_LORA
    o2 = o1 + MLA_KV_LORA
    o2r = o2 + MLA_ROPE
    half = MLA_ROPE // 2
    padl = lambda w, l, r: jnp.pad(w, ((0, 0), (l, r)))
    kpe = w_in[:, o2:o2r]
    kpe_sw = jnp.concatenate([-kpe[:, half:], kpe[:, :half]], axis=1)
    o3 = o2r + SWA_HEADS * SWA_HEAD_DIM
    o4 = o3 + SWA_KV_HEADS * SWA_HEAD_DIM
    g = SWA_HEADS // SWA_KV_HEADS
    swa_q = []
    for h in range(SWA_HEADS):
        hk = h // g
        blk = w_in[:, o2r + h * SWA_HEAD_DIM: o2r + (h + 1) * SWA_HEAD_DIM]
        swa_q.append(padl(blk, hk * SWA_HEAD_DIM, LANES - (hk + 1) * SWA_HEAD_DIM))
    win = jnp.concatenate(
        [w_in[:, :o2], padl(kpe, MLA_NOPE, LANES - MLA_NOPE - MLA_ROPE),
         padl(kpe_sw, MLA_NOPE, LANES - MLA_NOPE - MLA_ROPE)] + swa_q + [w_in[:, o3:o4], w_in[:, o4:]],
        axis=1).astype(BF16)
    dqk = MLA_NOPE + MLA_ROPE
    wq3 = w_qb.reshape(o1, MLA_HEADS, dqk)
    wq = jnp.pad(wq3, ((0, 0), (0, 0), (0, LANES - dqk))).reshape(o1, MLA_HEADS * LANES).astype(BF16)
    x1 = wq3[..., MLA_NOPE:MLA_NOPE + half]
    x2 = wq3[..., MLA_NOPE + half:]
    z = jnp.zeros_like
    wqs = jnp.concatenate([z(wq3[..., :MLA_NOPE]), -x2, x1, z(wq3[..., :LANES - dqk])], axis=-1)
    wqs = wqs.reshape(o1, MLA_HEADS * LANES).astype(BF16)
    wkv3 = w_kvb.reshape(MLA_KV_LORA, MLA_HEADS, MLA_NOPE + MLA_V)
    wk = jnp.pad(wkv3[..., :MLA_NOPE], ((0, 0), (0, 0), (0, LANES - MLA_NOPE)))
    wk = wk.reshape(MLA_KV_LORA, MLA_HEADS * LANES).astype(BF16)
    wv = jnp.pad(wkv3[..., MLA_NOPE:], ((0, 0), (0, 0), (0, LANES - MLA_V)))
    wv = wv.reshape(MLA_KV_LORA, MLA_HEADS * LANES).astype(BF16)
    return win, wq, wqs, wk, wv


def _rope_tables(T):
    half = MLA_ROPE // 2
    inv = 1.0 / (ROPE_THETA ** (jnp.arange(half, dtype=F32) / half))
    ang = jnp.arange(T, dtype=F32)[:, None] * inv[None, :]
    cos, sin = jnp.cos(ang), jnp.sin(ang)
    s = (MLA_NOPE + MLA_ROPE) ** -0.5 * math.log2(math.e)
    one = jnp.ones((T, MLA_NOPE), F32)
    zl = jnp.zeros((T, MLA_NOPE), F32)
    zr = jnp.zeros((T, LANES - MLA_NOPE - MLA_ROPE), F32)
    cq = jnp.concatenate([one * s, cos * s, cos * s, zr], axis=1)
    sq = jnp.concatenate([zl, sin * s, sin * s, zr], axis=1)
    ck = jnp.concatenate([zl, cos, cos, zr], axis=1)
    sk = jnp.concatenate([zl, sin, sin, zr], axis=1)
    return cq, sq, ck, sk


def l0_proj(x, mod, ng, win, qn, kvn, wq, wqs, wk, wv, tables):
    B, T, D = x.shape
    tm = _tile(T, 512)
    NP = win.shape[1]
    HQ = MLA_HEADS * LANES
    full = lambda a: pl.BlockSpec(a.shape, lambda b, i: (0,) * a.ndim)
    row = lambda n: pl.BlockSpec((None, tm, n), lambda b, i: (b, i, 0))
    tab = pl.BlockSpec((tm, LANES), lambda b, i: (i, 0))
    outs = [(HQ, BF16), (HQ, BF16), (HQ, BF16), (SWA_HEADS * LANES, BF16), (LANES, BF16), (LANES, BF16)]
    return pl.pallas_call(
        _l0_proj_kernel,
        grid=(B, T // tm),
        in_specs=[row(D), pl.BlockSpec((None, 6, D), lambda b, i: (b, 0, 0)), full(ng), full(win), full(qn),
                  full(kvn), full(wq), full(wqs), full(wk), full(wv), tab, tab, tab, tab],
        out_specs=[row(n) for n, _ in outs],
        out_shape=[jax.ShapeDtypeStruct((B, T, n), dt) for n, dt in outs],
        compiler_params=_cparams(("parallel", "parallel")),
        name="l0_proj",
    )(x, mod, ng, win, qn, kvn, wq, wqs, wk, wv, *tables)


def _mla_attn_kernel(q_ref, k_ref, v_ref, o_ref, m_ref, acc_ref, *, tk):
    T = k_ref.shape[0]
    nk = T // tk
    m_ref[...] = jnp.full(m_ref.shape, -jnp.inf, F32)
    acc_ref[...] = jnp.zeros(acc_ref.shape, F32)
    q = q_ref[...]
    per_trip = next(u for u in (8, 4, 2, 1) if nk % u == 0 and u * q_ref.shape[0] * tk * 4 <= SCORE_BYTES)

    def body(j, carry):
        r0 = [pl.multiple_of((j * per_trip + u) * tk, tk) for u in range(per_trip)]
        s = [_dot_nt(q, k_ref[pl.ds(r, tk), :]) for r in r0]
        m = m_ref[...]
        acc = acc_ref[...]
        for u in range(per_trip):
            m_new = jnp.maximum(m, jnp.max(s[u], axis=-1, keepdims=True))
            p = jnp.exp2(s[u] - jnp.tile(m_new, (1, tk // LANES)))
            acc = jnp.exp2(m - m_new) * acc + _dot(p.astype(BF16), v_ref[pl.ds(r0[u], tk), :])
            m = m_new
        m_ref[...] = m
        acc_ref[...] = acc
        return carry

    lax.fori_loop(0, nk // per_trip, body, 0)
    acc = acc_ref[...]
    o_ref[...] = (acc / acc[:, MLA_V:MLA_V + 1]).astype(o_ref.dtype)


def mla_attention(q, k, v):
    B, T, _ = q.shape
    tq = _tile(T, 1024)
    tk = _tile(T, 1024 if T >= 4096 else 512)
    qspec = pl.BlockSpec((None, tq, LANES), lambda b, h, i: (b, i, h))
    kspec = pl.BlockSpec((None, T, LANES), lambda b, h, i: (b, 0, h))
    return pl.pallas_call(
        functools.partial(_mla_attn_kernel, tk=tk),
        grid=(B, MLA_HEADS, T // tq),
        in_specs=[qspec, kspec, kspec],
        out_specs=qspec,
        out_shape=jax.ShapeDtypeStruct((B, T, MLA_HEADS * LANES), BF16),
        scratch_shapes=[pltpu.VMEM((tq, LANES), F32), pltpu.VMEM((tq, LANES), F32)],
        compiler_params=_cparams(("parallel", "parallel", "parallel")),
        name="mla_attention",
    )(q, k, v)


def _t5_bucket(rel):
    nb = REL_BUCKETS // 2
    max_exact = nb // 2
    n = jnp.abs(rel)
    big = max_exact + (jnp.log(jnp.maximum(n, 1).astype(F32) / max_exact)
                       / math.log(REL_MAX_DIST / max_exact) * (nb - max_exact)).astype(jnp.int32)
    big = jnp.minimum(big, nb - 1)
    return jnp.where(rel > 0, nb, 0) + jnp.where(n < max_exact, n, big)


def _swa_bias_kernel(rb_ref, bucket_ref, o_ref):
    h = pl.program_id(0)
    bucket = bucket_ref[...]
    acc = jnp.zeros(bucket.shape, F32)
    for b in range(REL_BUCKETS):
        acc = jnp.where(bucket == b, rb_ref[b * SWA_HEADS + h], acc)
    qi = lax.broadcasted_iota(jnp.int32, bucket.shape, 0)
    kj = lax.broadcasted_iota(jnp.int32, bucket.shape, 1)
    rel = kj - WINDOW - qi
    o_ref[...] = jnp.where(jnp.abs(rel) <= WINDOW, acc, -jnp.inf)


def swa_bias(rel_bias):
    W = WINDOW
    qi = jnp.arange(W, dtype=jnp.int32)[:, None]
    kj = jnp.arange(3 * W, dtype=jnp.int32)[None, :]
    bucket = _t5_bucket(kj - W - qi)
    return pl.pallas_call(
        _swa_bias_kernel,
        grid=(SWA_HEADS,),
        in_specs=[pl.BlockSpec(memory_space=pltpu.SMEM), pl.BlockSpec((W, 3 * W), lambda h: (0, 0))],
        out_specs=pl.BlockSpec((None, W, 3 * W), lambda h: (h, 0, 0)),
        out_shape=jax.ShapeDtypeStruct((SWA_HEADS, W, 3 * W), F32),
        name="swa_bias",
    )(rel_bias.reshape(-1), bucket)


def _swa_kernel(sink_ref, q_ref, kp_ref, kc_ref, kn_ref, vp_ref, vc_ref, vn_ref, bias_ref, o_ref):
    n = pl.program_id(1)
    nb = pl.num_programs(1)
    W = WINDOW
    kband = jnp.concatenate([kp_ref[...], kc_ref[...], kn_ref[...]], axis=0)
    vband = jnp.concatenate([vp_ref[...], vc_ref[...], vn_ref[...]], axis=0)
    col = lax.broadcasted_iota(jnp.int32, (W, 3 * W), 1)
    lo = jnp.where(n == 0, W, 0)
    hi = jnp.where(n == nb - 1, 2 * W, 3 * W)
    inside = (col >= lo) & (col < hi)
    lane = lax.broadcasted_iota(jnp.int32, (W, LANES), 1)
    G = SWA_HEADS // SWA_KV_HEADS
    heads = range(SWA_HEADS)
    s = [_dot_nt(q_ref[:, h * LANES:(h + 1) * LANES], kband) for h in heads]
    s = [jnp.where(inside, s[h] + bias_ref[h], -jnp.inf) for h in heads]
    m = [jnp.maximum(jnp.max(s[h], axis=-1, keepdims=True), sink_ref[h]) for h in heads]
    p = [jnp.exp(s[h] - m[h]) for h in heads]
    den = [jnp.sum(p[h], axis=-1, keepdims=True) + jnp.exp(sink_ref[h] - m[h]) for h in heads]
    o = [_dot(p[h].astype(BF16), vband) for h in heads]
    o = [o[h] / den[h] for h in heads]
    outs = [jnp.where(lane < SWA_HEAD_DIM, o[g], o[G + g]) for g in range(G)]
    o_ref[...] = jnp.concatenate(outs, axis=1).astype(o_ref.dtype)


def swa_attention(q, k, v, sink, bias):
    B, T, _ = q.shape
    W = WINDOW
    nb = T // W
    G = SWA_HEADS // SWA_KV_HEADS
    prev = pl.BlockSpec((None, W, LANES), lambda b, n: (b, jnp.maximum(n - 1, 0), 0))
    cur = pl.BlockSpec((None, W, LANES), lambda b, n: (b, n, 0))
    nxt = pl.BlockSpec((None, W, LANES), lambda b, n: (b, jnp.minimum(n + 1, nb - 1), 0))
    return pl.pallas_call(
        _swa_kernel,
        grid=(B, nb),
        in_specs=[pl.BlockSpec(memory_space=pltpu.SMEM),
                  pl.BlockSpec((None, W, SWA_HEADS * LANES), lambda b, n: (b, n, 0)),
                  prev, cur, nxt, prev, cur, nxt,
                  pl.BlockSpec(bias.shape, lambda b, n: (0, 0, 0))],
        out_specs=pl.BlockSpec((None, W, G * LANES), lambda b, n: (b, n, 0)),
        out_shape=jax.ShapeDtypeStruct((B, T, G * LANES), BF16),
        compiler_params=_cparams(("parallel", "parallel")),
        name="swa_attention",
    )(sink, q, k, k, k, v, v, v, bias)


def _mix_ffn_kernel(a_ref, b_ref, wa_ref, wb_ref, x_ref, mod_ref, ng_ref, wg_ref, wu_ref, wd_ref, o_ref):
    y = _dot(a_ref[...], wa_ref[...]) + _dot(b_ref[...], wb_ref[...])
    x = x_ref[...] + mod_ref[2:3, :] * _rms(y, ng_ref[1:2, :])
    h = _modulate(x, ng_ref[2:3, :], mod_ref[3:4, :], mod_ref[4:5, :]).astype(BF16)
    u = (_silu(_dot(h, wg_ref[...])) * _dot(h, wu_ref[...])).astype(BF16)
    o_ref[...] = x + mod_ref[5:6, :] * _rms(_dot(u, wd_ref[...]), ng_ref[3:4, :])


def mix_ffn(a, b, wa, wb, x, mod, ng, wg, wu, wd):
    B, T, D = x.shape
    tm = _tile(T, 512)
    row = lambda n: pl.BlockSpec((None, tm, n), lambda bb, i: (bb, i, 0))
    resident = lambda w: pl.BlockSpec(w.shape, lambda bb, i: (0, 0), pipeline_mode=pl.Buffered(1))
    return pl.pallas_call(
        _mix_ffn_kernel,
        grid=(B, T // tm),
        in_specs=[row(a.shape[-1]), row(b.shape[-1]), resident(wa), resident(wb), row(D),
                  pl.BlockSpec((None, 6, D), lambda bb, i: (bb, 0, 0)), pl.BlockSpec(ng.shape, lambda bb, i: (0, 0)),
                  resident(wg), resident(wu), resident(wd)],
        out_specs=row(D),
        out_shape=jax.ShapeDtypeStruct((B, T, D), F32),
        compiler_params=_cparams(("parallel", "parallel")),
        name="mix_ffn",
    )(a, b, wa, wb, x, mod, ng, wg, wu, wd)


def _l1_proj_kernel(x_ref, mod_ref, ng_ref, win_ref, wgate_ref, bgate_ref,
                    q_ref, k_ref, v_ref, r_ref, gf_ref, gb_ref):
    h = _modulate(x_ref[...], ng_ref[0:1, :], mod_ref[0:1, :], mod_ref[1:2, :]).astype(BF16)
    nk = q_ref.shape[-1]
    nv = v_ref.shape[-1]
    dk = nk // GLA_HEADS
    ng_cols = 2 * nk + 2 * nv
    gd = _dot(h, win_ref[:, ng_cols:]).astype(BF16)
    pre = _dot(gd, wgate_ref[...]) + bgate_ref[...]
    p = _dot(h, win_ref[:, :ng_cols])
    logsig = jnp.minimum(pre, 0.0) - jnp.log1p(jnp.exp(-jnp.abs(pre)))
    g = logsig / GLA_GATE_NORM
    gf_ref[...] = g[:, :nk]
    gb_ref[...] = g[:, nk:]
    q_ref[...] = p[:, :nk] * (dk ** -0.5)
    k_ref[...] = p[:, nk:2 * nk]
    v_ref[...] = p[:, 2 * nk:2 * nk + nv].astype(v_ref.dtype)
    r_ref[...] = p[:, 2 * nk + nv:].astype(r_ref.dtype)


def l1_proj(x, mod, ng, win, wgate, bgate, nk, nv):
    B, T, D = x.shape
    tm = _tile(T, 512)
    full = lambda a: pl.BlockSpec(a.shape, lambda b, i: (0,) * a.ndim)
    row = lambda n: pl.BlockSpec((None, tm, n), lambda b, i: (b, i, 0))
    outs = [(nk, F32), (nk, F32), (nv, BF16), (nv, BF16), (nk, F32), (nk, F32)]
    return pl.pallas_call(
        _l1_proj_kernel,
        grid=(B, T // tm),
        in_specs=[row(D), pl.BlockSpec((None, 6, D), lambda b, i: (b, 0, 0)), full(ng), full(win), full(wgate),
                  full(bgate)],
        out_specs=[row(n) for n, _ in outs],
        out_shape=[jax.ShapeDtypeStruct((B, T, n), dt) for n, dt in outs],
        compiler_params=_cparams(("parallel", "parallel")),
        name="l1_proj",
    )(x, mod, ng, win, wgate, bgate)


def _gla_kernel(q_ref, k_ref, g_ref, v_ref, o_ref, st_ref, *, reverse):
    t = pl.program_id(2)
    L = GLA_CHUNK
    nchunk = q_ref.shape[0] // L

    @pl.when(t == 0)
    def _():
        st_ref[...] = jnp.zeros(st_ref.shape, F32)

    ri = lax.broadcasted_iota(jnp.int32, (L, L), 0)
    ci = lax.broadcasted_iota(jnp.int32, (L, L), 1)
    keep = (ci >= ri) if reverse else (ci <= ri)
    order = list(range(nchunk - 1, -1, -1) if reverse else range(nchunk))
    chunks = [slice(c * L, (c + 1) * L) for c in range(nchunk)]

    b = g_ref[...]
    n = b.shape[0]
    pos = lax.broadcasted_iota(jnp.int32, b.shape, 0) & (L - 1)
    step = 1
    while step < L:
        if reverse:
            b = b + jnp.where(pos < L - step, pltpu.roll(b, n - step, axis=0), 0.0)
        else:
            b = b + jnp.where(pos >= step, pltpu.roll(b, step, axis=0), 0.0)
        step *= 2
    ends = [b[c * L:c * L + 1, :] if reverse else b[(c + 1) * L - 1:(c + 1) * L, :] for c in range(nchunk)]
    b_end = jnp.concatenate([jnp.broadcast_to(e, (L, e.shape[1])) for e in ends], axis=0)
    k = k_ref[...]
    qe = (q_ref[...] * jnp.exp(b)).astype(BF16)
    ke = (k * jnp.exp(-b)).astype(BF16)
    kd = (k * jnp.exp(b_end - b)).astype(BF16)
    v = v_ref[...]
    a = [_dot_nt(qe[r], ke[r]) for r in chunks]
    a = [jnp.where(keep, x, 0.0).astype(BF16) for x in a]
    intra = [_dot(a[c], v[chunks[c]]) for c in range(nchunk)]
    upd = [_dot_tn(v[r], kd[r]) for r in chunks]
    st = st_ref[...]
    outs = [None] * nchunk
    for c in order:
        outs[c] = (intra[c] + _dot_nt(qe[chunks[c]], st.astype(BF16))).astype(o_ref.dtype)
        st = st * jnp.exp(ends[c]) + upd[c]
    st_ref[...] = st
    o_ref[...] = jnp.concatenate(outs, axis=0)


def gla_scan(q, k, g, v, reverse):
    B, T, nk = q.shape
    nv = v.shape[-1]
    dk, dv = nk // GLA_HEADS, nv // GLA_HEADS
    tc = _tile(T, 1024)
    nt = T // tc
    blk = (lambda t: nt - 1 - t) if reverse else (lambda t: t)
    qspec = pl.BlockSpec((None, tc, dk), lambda b, h, t: (b, blk(t), h))
    vspec = pl.BlockSpec((None, tc, dv), lambda b, h, t: (b, blk(t), h))
    return pl.pallas_call(
        functools.partial(_gla_kernel, reverse=reverse),
        grid=(B, GLA_HEADS, nt),
        in_specs=[qspec, qspec, qspec, vspec],
        out_specs=vspec,
        out_shape=jax.ShapeDtypeStruct((B, T, nv), F32),
        scratch_shapes=[pltpu.VMEM((dv, dk), F32)],
        compiler_params=_cparams(("parallel", "parallel", "arbitrary")),
        name="gla_bwd" if reverse else "gla_fwd",
    )(q, k, g, v)


def _store_row_tiles(ref, x):
    m = x.shape[0]
    nc = x.shape[1] // LANES
    for c in range(nc):
        ref[pl.ds(c, m, stride=nc), :] = x[:, c * LANES:(c + 1) * LANES]


def _load_row_tiles(ref, nc):
    m = ref.shape[0] // nc
    return jnp.concatenate([ref[pl.ds(c, m, stride=nc), :] for c in range(nc)], axis=1)


def _gla_router_kernel(of_ref, ob_ref, r_ref, on_ref, w_ref, x_ref, mod_ref, ng_ref, whi_ref, wlo_ref,
                       x1_ref, h_ref, idx_ref, gate_ref):
    o = of_ref[...] + ob_ref[...]
    dv = on_ref.shape[-1]
    on = jnp.concatenate([_rms(o[:, hd * dv:(hd + 1) * dv], on_ref[...]) for hd in range(GLA_HEADS)], axis=1)
    y = _dot((on * _silu(r_ref[...].astype(F32))).astype(BF16), w_ref[...])
    x = x_ref[...] + mod_ref[2:3, :] * _rms(y, ng_ref[1:2, :])
    x1_ref[...] = x
    h = _modulate(x, ng_ref[2:3, :], mod_ref[3:4, :], mod_ref[4:5, :])
    _store_row_tiles(h_ref, h)
    h_hi = h.astype(BF16)
    h_lo = (h - h_hi.astype(F32)).astype(BF16)
    logits = _dot_nt(whi_ref[...], h_hi) + _dot_nt(whi_ref[...], h_lo) + _dot_nt(wlo_ref[...], h_hi)
    row = lax.broadcasted_iota(jnp.int32, logits.shape, 0)
    nrow = logits.shape[0]
    logits = jnp.where(row < N_EXPERTS, logits, -jnp.inf)
    m1 = jnp.max(logits, axis=0, keepdims=True)
    i1 = jnp.min(jnp.where(logits == m1, row, nrow), axis=0, keepdims=True)
    rest = jnp.where(row == i1, -jnp.inf, logits)
    m2 = jnp.max(rest, axis=0, keepdims=True)
    i2 = jnp.min(jnp.where(rest == m2, row, nrow), axis=0, keepdims=True)
    e = jnp.exp(m2 - m1)
    idx_ref[...] = jnp.concatenate([i1, i2], axis=0)
    gate_ref[...] = jnp.concatenate([1.0 / (1.0 + e), e / (1.0 + e)], axis=0)


def gla_router(o_f, o_b, r, out_norm, w_out, x, mod, ng, w_router):
    B, T, D = x.shape
    tm = _tile(T, 512)
    nv = o_f.shape[-1]
    wp = jnp.pad(w_router.T, ((0, BF16_SUBLANES - N_EXPERTS), (0, 0)))
    whi = wp.astype(BF16)
    wlo = (wp - whi.astype(F32)).astype(BF16)
    full = lambda w: pl.BlockSpec(w.shape, lambda bb, i: (0,) * w.ndim)
    row = lambda n: pl.BlockSpec((None, tm, n), lambda bb, i: (bb, i, 0))
    top = pl.BlockSpec((None, TOP_K, tm), lambda bb, i: (bb, 0, i))
    return pl.pallas_call(
        _gla_router_kernel,
        grid=(B, T // tm),
        in_specs=[row(nv), row(nv), row(nv), full(out_norm), full(w_out), row(D),
                  pl.BlockSpec((None, 6, D), lambda bb, i: (bb, 0, 0)), full(ng), full(whi), full(wlo)],
        out_specs=[row(D), pl.BlockSpec((None, tm * (D // LANES), LANES), lambda bb, i: (bb, i, 0)), top, top],
        out_shape=[jax.ShapeDtypeStruct((B, T, D), F32), jax.ShapeDtypeStruct((B, T * (D // LANES), LANES), F32),
                   jax.ShapeDtypeStruct((B, TOP_K, T), jnp.int32), jax.ShapeDtypeStruct((B, TOP_K, T), F32)],
        compiler_params=_cparams(("parallel", "parallel")),
        name="gla_router",
    )(o_f, o_b, r, out_norm, w_out, x, mod, ng, whi, wlo)


def _moe_kernel(blk_e_ref, tok_ref, dst_ref, h_hbm, wg_ref, wu_ref, wd_ref, out_hbm,
                xbuf, ybuf, gsem, ssem, *, bm, nh):
    i = pl.program_id(0)
    nb = pl.num_programs(0)
    slot = lax.rem(i, 2)
    other = 1 - slot
    nc = xbuf.shape[1] // bm

    def tile(row):
        return pl.ds(pl.multiple_of(row * nc, nc), nc)

    def gather(blk, r, s):
        return pltpu.make_async_copy(h_hbm.at[tile(tok_ref[blk * bm + r])], xbuf.at[s, tile(r)], gsem.at[s])

    def scatter(blk, r, s):
        return pltpu.make_async_copy(ybuf.at[s, tile(r)], out_hbm.at[tile(dst_ref[(blk + 1) * bm + r])], ssem.at[s])

    def wait_rows(buf, sem, s):
        pltpu.make_async_copy(h_hbm.at[pl.ds(0, bm * nc)], buf.at[s], sem.at[s]).wait()

    @pl.when(i == 0)
    def _():
        def start(r, c):
            gather(0, r, 0).start()
            return c

        lax.fori_loop(0, bm, start, 0, unroll=8)
        ybuf[1] = jnp.zeros(ybuf.shape[1:], F32)

    wait_rows(xbuf, gsem, slot)
    x = _load_row_tiles(xbuf.at[slot], nc).astype(BF16)

    nxt = jnp.minimum(i + 1, nb - 1)
    for r in range(bm):
        gather(nxt, r, other).start()
        scatter(i - 1, r, other).start()

    tf = wg_ref.shape[1] // nh
    y = None
    for c in range(nh):
        cols = slice(c * tf, (c + 1) * tf)
        a = (_silu(_dot(x, wg_ref[:, cols])) * _dot(x, wu_ref[:, cols])).astype(BF16)
        part = _dot(a, wd_ref[cols, :])
        y = part if y is None else y + part

    @pl.when(i >= 1)
    def _():
        wait_rows(ybuf, ssem, slot)

    _store_row_tiles(ybuf.at[slot], y)

    @pl.when(i == nb - 1)
    def _():
        def start(r, c):
            scatter(i, r, slot).start()
            return c

        lax.fori_loop(0, bm, start, 0, unroll=8)
        wait_rows(ybuf, ssem, slot)
        wait_rows(ybuf, ssem, other)
        wait_rows(xbuf, gsem, other)


def moe_experts(h, tok_buf, dst, blk_e, wg, wu, wd, bm, out_rows):
    D = wg.shape[1]
    C = D // LANES
    rows = tok_buf.shape[0]
    E, _, F = wg.shape
    nh = 2 if F % (4 * LANES) == 0 else 1
    wspec = lambda shape: pl.BlockSpec((None,) + shape, lambda i, be, tk, ds: (be[i], 0, 0),
                                       pipeline_mode=pl.Buffered(1))
    return pl.pallas_call(
        functools.partial(_moe_kernel, bm=bm, nh=nh),
        grid_spec=pltpu.PrefetchScalarGridSpec(
            num_scalar_prefetch=3,
            grid=(rows // bm,),
            in_specs=[pl.BlockSpec(memory_space=pl.ANY), wspec((D, F)), wspec((D, F)), wspec((F, D))],
            out_specs=pl.BlockSpec(memory_space=pl.ANY),
            scratch_shapes=[pltpu.VMEM((2, bm * C, LANES), F32), pltpu.VMEM((2, bm * C, LANES), F32),
                            pltpu.SemaphoreType.DMA((2,)), pltpu.SemaphoreType.DMA((2,))],
        ),
        out_shape=jax.ShapeDtypeStruct((out_rows * C, LANES), F32),
        compiler_params=_cparams(("arbitrary",)),
        name="moe_experts",
    )(blk_e, tok_buf, dst, h, wg, wu, wd)


def _moe_combine_kernel(y0_ref, y1_ref, gate_ref, x_ref, mod_ref, ng_ref, o_ref):
    gate = gate_ref[...]
    nc = x_ref.shape[-1] // LANES
    y = _load_row_tiles(y0_ref, nc) * gate[:, 0:1] + _load_row_tiles(y1_ref, nc) * gate[:, 1:2]
    o_ref[...] = x_ref[...] + mod_ref[5:6, :] * _rms(y, ng_ref[3:4, :])


def moe_combine(y, gates, x, mod, ng):
    B, T, D = x.shape
    tm = _tile(T, 512)
    nt = T // tm
    yspec = lambda k: pl.BlockSpec((tm * (D // LANES), LANES), lambda b, i: (k * B * nt + b * nt + i, 0))
    row = lambda n: pl.BlockSpec((None, tm, n), lambda b, i: (b, i, 0))
    return pl.pallas_call(
        _moe_combine_kernel,
        grid=(B, nt),
        in_specs=[yspec(0), yspec(1), row(TOP_K), row(D), pl.BlockSpec((None, 6, D), lambda b, i: (b, 0, 0)),
                  pl.BlockSpec(ng.shape, lambda b, i: (0, 0))],
        out_specs=row(D),
        out_shape=jax.ShapeDtypeStruct((B, T, D), F32),
        compiler_params=_cparams(("parallel", "parallel")),
        name="moe_combine",
    )(y, y, gates, x, mod, ng)


def _routing_tables(top_idx, bm):
    n = top_idx.shape[0]
    flat_e = top_idx.reshape(-1)
    onehot = (flat_e[:, None] == jnp.arange(N_EXPERTS, dtype=jnp.int32)[None, :]).astype(jnp.int32)
    csum = jnp.cumsum(onehot, axis=0)
    rank = jnp.sum(csum * onehot, axis=1) - 1
    counts = csum[-1]
    padded = (counts + bm - 1) // bm * bm
    pad_end = jnp.cumsum(padded)
    pad_start = pad_end - padded
    pos = jnp.sum(pad_start[None, :] * onehot, axis=1) + rank
    rows = n * TOP_K + N_EXPERTS * bm
    flat = jnp.arange(n * TOP_K, dtype=jnp.int32)
    src = jnp.full((rows,), -1, jnp.int32).at[pos].set(flat, unique_indices=True)
    is_pad = src < 0
    tok_buf = jnp.where(is_pad, 0, src // TOP_K)
    spare = n * TOP_K + jnp.cumsum(is_pad.astype(jnp.int32)) - 1
    dst = jnp.where(is_pad, spare, (src % TOP_K) * n + src // TOP_K)
    dst = jnp.concatenate([rows + jnp.arange(bm, dtype=jnp.int32), dst])
    blk_start = jnp.arange(rows // bm, dtype=jnp.int32) * bm
    blk_e = jnp.minimum(jnp.searchsorted(pad_end, blk_start, side='right'), N_EXPERTS - 1).astype(jnp.int32)
    return tok_buf, dst.astype(jnp.int32), blk_e, rows + bm


def gla_out_moe(o_f, o_b, r, out_norm, w_out, x, mod, ng, w_router, wg, wu, wd, bm):
    B, T, D = x.shape
    x, h, top_idx, gates = gla_router(o_f, o_b, r, out_norm, w_out, x, mod, ng, w_router)
    top_idx = jnp.swapaxes(top_idx, 1, 2)
    gates = jnp.swapaxes(gates, 1, 2)
    tok_buf, dst, blk_e, out_rows = _routing_tables(top_idx.reshape(B * T, TOP_K), bm)
    y = moe_experts(h.reshape(-1, LANES), tok_buf, dst, blk_e, wg, wu, wd, bm, out_rows)
    return moe_combine(y, gates, x, mod, ng)


def _prep_weights(p):
    w = {}
    w['l0'] = _prep_l0_weights(p['w_in_ab'][0], p['mla_w_qb'][0], p['mla_w_kvb'][0])
    w['qn'] = p['mla_q_norm'][0][None, :]
    w['kvn'] = p['mla_kv_norm'][0][None, :]
    w_out = p['w_out_ab'][0]
    na = MLA_HEADS * MLA_V
    g = SWA_HEADS // SWA_KV_HEADS
    D = w_out.shape[1]
    wa = jnp.pad(w_out[:na].reshape(MLA_HEADS, MLA_V, D), ((0, 0), (0, LANES - MLA_V), (0, 0)))
    w['wa'] = wa.reshape(MLA_HEADS * LANES, D).astype(BF16)
    w['wb'] = w_out[na:].reshape(SWA_KV_HEADS, g, SWA_HEAD_DIM, D).transpose(1, 0, 2, 3).reshape(-1, D).astype(BF16)
    w['ffn'] = tuple(p[k][0].astype(BF16) for k in ('ffn_w_gate', 'ffn_w_up', 'ffn_w_down'))
    w_in_c = p['w_in_c'][0]
    ncol = w_in_c.shape[1]
    w['win_c'] = jnp.pad(w_in_c, ((0, 0), (0, LANES - 2 * GLA_GATE_RANK))).astype(BF16)
    wgu = p['gla_w_gate_up'][0]
    nk = wgu.shape[-1]
    zero = jnp.zeros((GLA_GATE_RANK, nk), F32)
    wgate = jnp.concatenate([jnp.concatenate([wgu[0], zero], axis=1), jnp.concatenate([zero, wgu[1]], axis=1)], axis=0)
    w['wgate'] = jnp.pad(wgate, ((0, LANES - 2 * GLA_GATE_RANK), (0, 0))).astype(BF16)
    w['bgate'] = p['gla_b_gate_up'][0].reshape(1, 2 * nk)
    w['nk'] = nk
    w['nv'] = (ncol - 2 * nk - 2 * GLA_GATE_RANK) // 2
    w['out_norm'] = p['gla_out_norm'][0][None, :]
    w['w_out_c'] = p['w_out_c'][0].astype(BF16)
    w['router'] = p['moe_router'][0]
    w['moe'] = tuple(p[k][0].astype(BF16) for k in ('moe_w_gate', 'moe_w_up', 'moe_w_down'))
    w['swa_bias'] = swa_bias(p['rel_bias'])
    w['sink'] = p['swa_sink'][0]
    return w


def _trunk(x, c, p, w, moe_bm):
    B, T, D = x.shape
    tables = _rope_tables(T)
    mod = ada_mod(c, p['w_ada'], p['b_ada'], 0)
    ng = p['norm_gain'][0]
    q, k, v, sq, sk, sv = l0_proj(x, mod, ng, w['l0'][0], w['qn'], w['kvn'], *w['l0'][1:], tables)
    a = mla_attention(q, k, v)
    b = swa_attention(sq, sk, sv, w['sink'], w['swa_bias'])
    x = mix_ffn(a, b, w['wa'], w['wb'], x, mod, ng, *w['ffn'])
    mod = ada_mod(c, p['w_ada'], p['b_ada'], 1)
    ng = p['norm_gain'][1]
    q, k, v, r, gf, gb = l1_proj(x, mod, ng, w['win_c'], w['wgate'], w['bgate'], w['nk'], w['nv'])
    o_f = gla_scan(q, k, gf, v, reverse=False)
    o_b = gla_scan(q, k, gb, v, reverse=True)
    return gla_out_moe(o_f, o_b, r, w['out_norm'], w['w_out_c'], x, mod, ng, w['router'], *w['moe'], moe_bm)


def kernel(x_prompt, x_sample, c_prompt, c_sample, norm_gain, w_ada, b_ada, rel_bias, w_in_ab, mla_q_norm, mla_kv_norm, mla_w_qb, mla_w_kvb, swa_sink, w_out_ab, ffn_w_gate, ffn_w_up, ffn_w_down, w_in_c, gla_w_gate_up, gla_b_gate_up, gla_out_norm, w_out_c, moe_router, moe_w_gate, moe_w_up, moe_w_down):
    p = dict(norm_gain=norm_gain, w_ada=w_ada, b_ada=b_ada, rel_bias=rel_bias, w_in_ab=w_in_ab,
             mla_q_norm=mla_q_norm, mla_kv_norm=mla_kv_norm, mla_w_qb=mla_w_qb, mla_w_kvb=mla_w_kvb,
             swa_sink=swa_sink, w_out_ab=w_out_ab, ffn_w_gate=ffn_w_gate, ffn_w_up=ffn_w_up,
             ffn_w_down=ffn_w_down, w_in_c=w_in_c, gla_w_gate_up=gla_w_gate_up, gla_b_gate_up=gla_b_gate_up,
             gla_out_norm=gla_out_norm, w_out_c=w_out_c, moe_router=moe_router, moe_w_gate=moe_w_gate,
             moe_w_up=moe_w_up, moe_w_down=moe_w_down)
    w = _prep_weights(p)
    bm = 512
    return (_trunk(x_prompt, c_prompt, p, w, bm), _trunk(x_sample, c_sample, p, w, bm))
```

```python
import functools
import math

import jax
import jax.numpy as jnp
from jax import lax
from jax.experimental import pallas as pl
from jax.experimental.pallas import tpu as pltpu

F32 = jnp.float32
BF16 = jnp.bfloat16

MLA_HEADS = 8
MLA_NOPE = 64
MLA_ROPE = 32
MLA_V = 64
MLA_Q_LORA = 384
MLA_KV_LORA = 256
ROPE_THETA = 10000.0
SWA_HEADS = 8
SWA_KV_HEADS = 2
SWA_HEAD_DIM = 64
WINDOW = 128
REL_BUCKETS = 32
REL_MAX_DIST = 128
GLA_HEADS = 4
GLA_GATE_RANK = 16
GLA_GATE_NORM = 16.0
GLA_CHUNK = 64
N_EXPERTS = 8
TOP_K = 2
EPS = 1e-6

LANES = 128
BF16_SUBLANES = 16
VMEM_LIMIT = 56 * 1024 * 1024
SCORE_BYTES = 16 * 1024 * 1024


def _cparams(sem, vmem=VMEM_LIMIT):
    return pltpu.CompilerParams(dimension_semantics=sem, vmem_limit_bytes=vmem)


def _tile(n, pref):
    if n <= pref:
        return n
    t = pref
    while n % t:
        t -= 8
    return t


def _rms(x, g):
    return x * lax.rsqrt(jnp.mean(x * x, axis=-1, keepdims=True) + EPS) * g


def _modulate(x, g, shift, scale):
    return _rms(x, g) * (1.0 + scale) + shift


def _silu(x):
    return x / (1.0 + jnp.exp(-x))


def _dot(a, b):
    return jnp.dot(a, b, preferred_element_type=F32)


def _dot_nt(a, b):
    return lax.dot_general(a, b, (((1,), (1,)), ((), ())), preferred_element_type=F32)


def _dot_tn(a, b):
    return lax.dot_general(a, b, (((0,), (0,)), ((), ())), preferred_element_type=F32)


def _ada_kernel(c_ref, w_ref, b_ref, o_ref):
    s = _silu(c_ref[...]).astype(BF16)
    o_ref[...] = _dot(s, w_ref[...].astype(BF16)) + b_ref[...]


def ada_mod(c, w_ada, b_ada, layer):
    B, D = c.shape
    N = w_ada.shape[-1]
    Bp = -(-B // 16) * 16
    cp = jnp.pad(c, ((0, Bp - B), (0, 0)))
    tn = _tile(N, 1536)
    out = pl.pallas_call(
        _ada_kernel,
        grid=(N // tn,),
        in_specs=[pl.BlockSpec((Bp, D), lambda j: (0, 0)),
                  pl.BlockSpec((None, D, tn), lambda j: (layer, 0, j)),
                  pl.BlockSpec((None, 1, tn), lambda j: (layer, 0, j))],
        out_specs=pl.BlockSpec((Bp, tn), lambda j: (0, j)),
        out_shape=jax.ShapeDtypeStruct((Bp, N), F32),
        compiler_params=_cparams(("parallel",)),
        name="ada_mod",
    )(cp, w_ada, b_ada.reshape(b_ada.shape[0], 1, N))
    return out[:B].reshape(B, 6, D)


def _l0_proj_kernel(x_ref, mod_ref, ng_ref, win_ref, qn_ref, kvn_ref, wq_ref, wqs_ref, wk_ref, wv_ref,
                    cq_ref, sq_ref, ck_ref, sk_ref,
                    q_ref, k_ref, v_ref, sq_out_ref, sk_out_ref, sv_out_ref):
    h = _modulate(x_ref[...], ng_ref[0:1, :], mod_ref[0:1, :], mod_ref[1:2, :]).astype(BF16)
    p = _dot(h, win_ref[...])
    o1 = MLA_Q_LORA
    o2 = o1 + MLA_KV_LORA
    nq = _rms(p[:, :o1], qn_ref[...]).astype(BF16)
    q = _dot(nq, wq_ref[...])
    qs = _dot(nq, wqs_ref[...])
    cq = jnp.concatenate([cq_ref[...]] * MLA_HEADS, axis=1)
    sq = jnp.concatenate([sq_ref[...]] * MLA_HEADS, axis=1)
    q_ref[...] = (q * cq + qs * sq).astype(q_ref.dtype)
    nkv = _rms(p[:, o1:o2], kvn_ref[...]).astype(BF16)
    kpe = p[:, o2:o2 + LANES] * ck_ref[...] + p[:, o2 + LANES:o2 + 2 * LANES] * sk_ref[...]
    k = _dot(nkv, wk_ref[...]) + jnp.concatenate([kpe] * MLA_HEADS, axis=1)
    k_ref[...] = k.astype(k_ref.dtype)
    v = _dot(nkv, wv_ref[...])
    lane = lax.broadcasted_iota(jnp.int32, v.shape, 1)
    v_ref[...] = jnp.where((lane & (LANES - 1)) == MLA_V, 1.0, v).astype(v_ref.dtype)
    o3 = o2 + 2 * LANES
    o4 = o3 + SWA_HEADS * LANES
    sq_out_ref[...] = (p[:, o3:o4] * (SWA_HEAD_DIM ** -0.5)).astype(sq_out_ref.dtype)
    sk_out_ref[...] = p[:, o4:o4 + LANES].astype(sk_out_ref.dtype)
    sv_out_ref[...] = p[:, o4 + LANES:o4 + 2 * LANES].astype(sv_out_ref.dtype)


def _prep_l0_weights(w_in, w_qb, w_kvb):
    D = w_in.shape[0]
    o1 = MLA_Q_LORA
    o2 = o1 + MLA_KV_LORA
    o2r = o2 + MLA_ROPE
    half = MLA_ROPE // 2
    padl = lambda w, l, r: jnp.pad(w, ((0, 0), (l, r)))
    kpe = w_in[:, o2:o2r]
    kpe_sw = jnp.concatenate([-kpe[:, half:], kpe[:, :half]], axis=1)
    o3 = o2r + SWA_HEADS * SWA_HEAD_DIM
    o4 = o3 + SWA_KV_HEADS * SWA_HEAD_DIM
    g = SWA_HEADS // SWA_KV_HEADS
    swa_q = []
    for h in range(SWA_HEADS):
        hk = h // g
        blk = w_in[:, o2r + h * SWA_HEAD_DIM: o2r + (h + 1) * SWA_HEAD_DIM]
        swa_q.append(padl(blk, hk * SWA_HEAD_DIM, LANES - (hk + 1) * SWA_HEAD_DIM))
    win = jnp.concatenate(
        [w_in[:, :o2], padl(kpe, MLA_NOPE, LANES - MLA_NOPE - MLA_ROPE),
         padl(kpe_sw, MLA_NOPE, LANES - MLA_NOPE - MLA_ROPE)] + swa_q + [w_in[:, o3:o4], w_in[:, o4:]],
        axis=1).astype(BF16)
    dqk = MLA_NOPE + MLA_ROPE
    wq3 = w_qb.reshape(o1, MLA_HEADS, dqk)
    wq = jnp.pad(wq3, ((0, 0), (0, 0), (0, LANES - dqk))).reshape(o1, MLA_HEADS * LANES).astype(BF16)
    x1 = wq3[..., MLA_NOPE:MLA_NOPE + half]
    x2 = wq3[..., MLA_NOPE + half:]
    z = jnp.zeros_like
    wqs = jnp.concatenate([z(wq3[..., :MLA_NOPE]), -x2, x1, z(wq3[..., :LANES - dqk])], axis=-1)
    wqs = wqs.reshape(o1, MLA_HEADS * LANES).astype(BF16)
    wkv3 = w_kvb.reshape(MLA_KV_LORA, MLA_HEADS, MLA_NOPE + MLA_V)
    wk = jnp.pad(wkv3[..., :MLA_NOPE], ((0, 0), (0, 0), (0, LANES - MLA_NOPE)))
    wk = wk.reshape(MLA_KV_LORA, MLA_HEADS * LANES).astype(BF16)
    wv = jnp.pad(wkv3[..., MLA_NOPE:], ((0, 0), (0, 0), (0, LANES - MLA_V)))
    wv = wv.reshape(MLA_KV_LORA, MLA_HEADS * LANES).astype(BF16)
    return win, wq, wqs, wk, wv


def _rope_tables(T):
    half = MLA_ROPE // 2
    inv = 1.0 / (ROPE_THETA ** (jnp.arange(half, dtype=F32) / half))
    ang = jnp.arange(T, dtype=F32)[:, None] * inv[None, :]
    cos, sin = jnp.cos(ang), jnp.sin(ang)
    s = (MLA_NOPE + MLA_ROPE) ** -0.5 * math.log2(math.e)
    one = jnp.ones((T, MLA_NOPE), F32)
    zl = jnp.zeros((T, MLA_NOPE), F32)
    zr = jnp.zeros((T, LANES - MLA_NOPE - MLA_ROPE), F32)
    cq = jnp.concatenate([one * s, cos * s, cos * s, zr], axis=1)
    sq = jnp.concatenate([zl, sin * s, sin * s, zr], axis=1)
    ck = jnp.concatenate([zl, cos, cos, zr], axis=1)
    sk = jnp.concatenate([zl, sin, sin, zr], axis=1)
    return cq, sq, ck, sk


def l0_proj(x, mod, ng, win, qn, kvn, wq, wqs, wk, wv, tables):
    B, T, D = x.shape
    tm = _tile(T, 512)
    NP = win.shape[1]
    HQ = MLA_HEADS * LANES
    full = lambda a: pl.BlockSpec(a.shape, lambda b, i: (0,) * a.ndim)
    row = lambda n: pl.BlockSpec((None, tm, n), lambda b, i: (b, i, 0))
    tab = pl.BlockSpec((tm, LANES), lambda b, i: (i, 0))
    outs = [(HQ, BF16), (HQ, BF16), (HQ, BF16), (SWA_HEADS * LANES, BF16), (LANES, BF16), (LANES, BF16)]
    return pl.pallas_call(
        _l0_proj_kernel,
        grid=(B, T // tm),
        in_specs=[row(D), pl.BlockSpec((None, 6, D), lambda b, i: (b, 0, 0)), full(ng), full(win), full(qn),
                  full(kvn), full(wq), full(wqs), full(wk), full(wv), tab, tab, tab, tab],
        out_specs=[row(n) for n, _ in outs],
        out_shape=[jax.ShapeDtypeStruct((B, T, n), dt) for n, dt in outs],
        compiler_params=_cparams(("parallel", "parallel")),
        name="l0_proj",
    )(x, mod, ng, win, qn, kvn, wq, wqs, wk, wv, *tables)


def _mla_attn_kernel(q_ref, k_ref, v_ref, o_ref, m_ref, acc_ref, *, tk):
    T = k_ref.shape[0]
    nk = T // tk
    m_ref[...] = jnp.full(m_ref.shape, -jnp.inf, F32)
    acc_ref[...] = jnp.zeros(acc_ref.shape, F32)
    q = q_ref[...]
    per_trip = next(u for u in (8, 4, 2, 1) if nk % u == 0 and u * q_ref.shape[0] * tk * 4 <= SCORE_BYTES)

    def body(j, carry):
        r0 = [pl.multiple_of((j * per_trip + u) * tk, tk) for u in range(per_trip)]
        s = [_dot_nt(q, k_ref[pl.ds(r, tk), :]) for r in r0]
        m = m_ref[...]
        acc = acc_ref[...]
        for u in range(per_trip):
            m_new = jnp.maximum(m, jnp.max(s[u], axis=-1, keepdims=True))
            p = jnp.exp2(s[u] - jnp.tile(m_new, (1, tk // LANES)))
            acc = jnp.exp2(m - m_new) * acc + _dot(p.astype(BF16), v_ref[pl.ds(r0[u], tk), :])
            m = m_new
        m_ref[...] = m
        acc_ref[...] = acc
        return carry

    lax.fori_loop(0, nk // per_trip, body, 0)
    acc = acc_ref[...]
    o_ref[...] = (acc / acc[:, MLA_V:MLA_V + 1]).astype(o_ref.dtype)


def mla_attention(q, k, v):
    B, T, _ = q.shape
    tq = _tile(T, 1024)
    tk = _tile(T, 1024 if T >= 4096 else 512)
    qspec = pl.BlockSpec((None, tq, LANES), lambda b, h, i: (b, i, h))
    kspec = pl.BlockSpec((None, T, LANES), lambda b, h, i: (b, 0, h))
    return pl.pallas_call(
        functools.partial(_mla_attn_kernel, tk=tk),
        grid=(B, MLA_HEADS, T // tq),
        in_specs=[qspec, kspec, kspec],
        out_specs=qspec,
        out_shape=jax.ShapeDtypeStruct((B, T, MLA_HEADS * LANES), BF16),
        scratch_shapes=[pltpu.VMEM((tq, LANES), F32), pltpu.VMEM((tq, LANES), F32)],
        compiler_params=_cparams(("parallel", "parallel", "parallel")),
        name="mla_attention",
    )(q, k, v)


def _t5_bucket(rel):
    nb = REL_BUCKETS // 2
    max_exact = nb // 2
    n = jnp.abs(rel)
    big = max_exact + (jnp.log(jnp.maximum(n, 1).astype(F32) / max_exact)
                       / math.log(REL_MAX_DIST / max_exact) * (nb - max_exact)).astype(jnp.int32)
    big = jnp.minimum(big, nb - 1)
    return jnp.where(rel > 0, nb, 0) + jnp.where(n < max_exact, n, big)


def _swa_bias_kernel(rb_ref, bucket_ref, o_ref):
    h = pl.program_id(0)
    bucket = bucket_ref[...]
    acc = jnp.zeros(bucket.shape, F32)
    for b in range(REL_BUCKETS):
        acc = jnp.where(bucket == b, rb_ref[b * SWA_HEADS + h], acc)
    qi = lax.broadcasted_iota(jnp.int32, bucket.shape, 0)
    kj = lax.broadcasted_iota(jnp.int32, bucket.shape, 1)
    rel = kj - WINDOW - qi
    o_ref[...] = jnp.where(jnp.abs(rel) <= WINDOW, acc, -jnp.inf)


def swa_bias(rel_bias):
    W = WINDOW
    qi = jnp.arange(W, dtype=jnp.int32)[:, None]
    kj = jnp.arange(3 * W, dtype=jnp.int32)[None, :]
    bucket = _t5_bucket(kj - W - qi)
    return pl.pallas_call(
        _swa_bias_kernel,
        grid=(SWA_HEADS,),
        in_specs=[pl.BlockSpec(memory_space=pltpu.SMEM), pl.BlockSpec((W, 3 * W), lambda h: (0, 0))],
        out_specs=pl.BlockSpec((None, W, 3 * W), lambda h: (h, 0, 0)),
        out_shape=jax.ShapeDtypeStruct((SWA_HEADS, W, 3 * W), F32),
        name="swa_bias",
    )(rel_bias.reshape(-1), bucket)


def _swa_kernel(sink_ref, q_ref, kp_ref, kc_ref, kn_ref, vp_ref, vc_ref, vn_ref, bias_ref, o_ref):
    n = pl.program_id(1)
    nb = pl.num_programs(1)
    W = WINDOW
    kband = jnp.concatenate([kp_ref[...], kc_ref[...], kn_ref[...]], axis=0)
    vband = jnp.concatenate([vp_ref[...], vc_ref[...], vn_ref[...]], axis=0)
    col = lax.broadcasted_iota(jnp.int32, (W, 3 * W), 1)
    lo = jnp.where(n == 0, W, 0)
    hi = jnp.where(n == nb - 1, 2 * W, 3 * W)
    inside = (col >= lo) & (col < hi)
    lane = lax.broadcasted_iota(jnp.int32, (W, LANES), 1)
    G = SWA_HEADS // SWA_KV_HEADS
    heads = range(SWA_HEADS)
    s = [_dot_nt(q_ref[:, h * LANES:(h + 1) * LANES], kband) for h in heads]
    s = [jnp.where(inside, s[h] + bias_ref[h], -jnp.inf) for h in heads]
    m = [jnp.maximum(jnp.max(s[h], axis=-1, keepdims=True), sink_ref[h]) for h in heads]
    p = [jnp.exp(s[h] - m[h]) for h in heads]
    den = [jnp.sum(p[h], axis=-1, keepdims=True) + jnp.exp(sink_ref[h] - m[h]) for h in heads]
    o = [_dot(p[h].astype(BF16), vband) for h in heads]
    o = [o[h] / den[h] for h in heads]
    outs = [jnp.where(lane < SWA_HEAD_DIM, o[g], o[G + g]) for g in range(G)]
    o_ref[...] = jnp.concatenate(outs, axis=1).astype(o_ref.dtype)


def swa_attention(q, k, v, sink, bias):
    B, T, _ = q.shape
    W = WINDOW
    nb = T // W
    G = SWA_HEADS // SWA_KV_HEADS
    prev = pl.BlockSpec((None, W, LANES), lambda b, n: (b, jnp.maximum(n - 1, 0), 0))
    cur = pl.BlockSpec((None, W, LANES), lambda b, n: (b, n, 0))
    nxt = pl.BlockSpec((None, W, LANES), lambda b, n: (b, jnp.minimum(n + 1, nb - 1), 0))
    return pl.pallas_call(
        _swa_kernel,
        grid=(B, nb),
        in_specs=[pl.BlockSpec(memory_space=pltpu.SMEM),
                  pl.BlockSpec((None, W, SWA_HEADS * LANES), lambda b, n: (b, n, 0)),
                  prev, cur, nxt, prev, cur, nxt,
                  pl.BlockSpec(bias.shape, lambda b, n: (0, 0, 0))],
        out_specs=pl.BlockSpec((None, W, G * LANES), lambda b, n: (b, n, 0)),
        out_shape=jax.ShapeDtypeStruct((B, T, G * LANES), BF16),
        compiler_params=_cparams(("parallel", "parallel")),
        name="swa_attention",
    )(sink, q, k, k, k, v, v, v, bias)


def _mix_ffn_kernel(a_ref, b_ref, wa_ref, wb_ref, x_ref, mod_ref, ng_ref, wg_ref, wu_ref, wd_ref, o_ref):
    y = _dot(a_ref[...], wa_ref[...]) + _dot(b_ref[...], wb_ref[...])
    x = x_ref[...] + mod_ref[2:3, :] * _rms(y, ng_ref[1:2, :])
    h = _modulate(x, ng_ref[2:3, :], mod_ref[3:4, :], mod_ref[4:5, :]).astype(BF16)
    u = (_silu(_dot(h, wg_ref[...])) * _dot(h, wu_ref[...])).astype(BF16)
    o_ref[...] = x + mod_ref[5:6, :] * _rms(_dot(u, wd_ref[...]), ng_ref[3:4, :])


def mix_ffn(a, b, wa, wb, x, mod, ng, wg, wu, wd):
    B, T, D = x.shape
    tm = _tile(T, 512)
    row = lambda n: pl.BlockSpec((None, tm, n), lambda bb, i: (bb, i, 0))
    resident = lambda w: pl.BlockSpec(w.shape, lambda bb, i: (0, 0), pipeline_mode=pl.Buffered(1))
    return pl.pallas_call(
        _mix_ffn_kernel,
        grid=(B, T // tm),
        in_specs=[row(a.shape[-1]), row(b.shape[-1]), resident(wa), resident(wb), row(D),
                  pl.BlockSpec((None, 6, D), lambda bb, i: (bb, 0, 0)), pl.BlockSpec(ng.shape, lambda bb, i: (0, 0)),
                  resident(wg), resident(wu), resident(wd)],
        out_specs=row(D),
        out_shape=jax.ShapeDtypeStruct((B, T, D), F32),
        compiler_params=_cparams(("parallel", "parallel")),
        name="mix_ffn",
    )(a, b, wa, wb, x, mod, ng, wg, wu, wd)


def _l1_proj_kernel(x_ref, mod_ref, ng_ref, win_ref, wgate_ref, bgate_ref,
                    q_ref, k_ref, v_ref, r_ref, gf_ref, gb_ref):
    h = _modulate(x_ref[...], ng_ref[0:1, :], mod_ref[0:1, :], mod_ref[1:2, :]).astype(BF16)
    nk = q_ref.shape[-1]
    nv = v_ref.shape[-1]
    dk = nk // GLA_HEADS
    ng_cols = 2 * nk + 2 * nv
    gd = _dot(h, win_ref[:, ng_cols:]).astype(BF16)
    pre = _dot(gd, wgate_ref[...]) + bgate_ref[...]
    p = _dot(h, win_ref[:, :ng_cols])
    logsig = jnp.minimum(pre, 0.0) - jnp.log1p(jnp.exp(-jnp.abs(pre)))
    g = logsig / GLA_GATE_NORM
    gf_ref[...] = g[:, :nk]
    gb_ref[...] = g[:, nk:]
    q_ref[...] = p[:, :nk] * (dk ** -0.5)
    k_ref[...] = p[:, nk:2 * nk]
    v_ref[...] = p[:, 2 * nk:2 * nk + nv].astype(v_ref.dtype)
    r_ref[...] = p[:, 2 * nk + nv:].astype(r_ref.dtype)


def l1_proj(x, mod, ng, win, wgate, bgate, nk, nv):
    B, T, D = x.shape
    tm = _tile(T, 512)
    full = lambda a: pl.BlockSpec(a.shape, lambda b, i: (0,) * a.ndim)
    row = lambda n: pl.BlockSpec((None, tm, n), lambda b, i: (b, i, 0))
    outs = [(nk, F32), (nk, F32), (nv, BF16), (nv, BF16), (nk, F32), (nk, F32)]
    return pl.pallas_call(
        _l1_proj_kernel,
        grid=(B, T // tm),
        in_specs=[row(D), pl.BlockSpec((None, 6, D), lambda b, i: (b, 0, 0)), full(ng), full(win), full(wgate),
                  full(bgate)],
        out_specs=[row(n) for n, _ in outs],
        out_shape=[jax.ShapeDtypeStruct((B, T, n), dt) for n, dt in outs],
        compiler_params=_cparams(("parallel", "parallel")),
        name="l1_proj",
    )(x, mod, ng, win, wgate, bgate)


def _gla_kernel(qf_ref, kf_ref, gf_ref, vf_ref, qb_ref, kb_ref, gb_ref, vb_ref, of_ref, ob_ref, stf_ref, stb_ref):
    t = pl.program_id(2)
    L = GLA_CHUNK
    nchunk = qf_ref.shape[0] // L

    @pl.when(t == 0)
    def _():
        stf_ref[...] = jnp.zeros(stf_ref.shape, F32)
        stb_ref[...] = jnp.zeros(stb_ref.shape, F32)

    ri = lax.broadcasted_iota(jnp.int32, (L, L), 0)
    ci = lax.broadcasted_iota(jnp.int32, (L, L), 1)
    chunks = [slice(c * L, (c + 1) * L) for c in range(nchunk)]
    dirs = [dict(rev=False, q=qf_ref, k=kf_ref, g=gf_ref, v=vf_ref, o=of_ref, st=stf_ref, keep=ci <= ri),
            dict(rev=True, q=qb_ref, k=kb_ref, g=gb_ref, v=vb_ref, o=ob_ref, st=stb_ref, keep=ci >= ri)]

    for d in dirs:
        b = d['g'][...]
        n = b.shape[0]
        pos = lax.broadcasted_iota(jnp.int32, b.shape, 0) & (L - 1)
        step = 1
        while step < L:
            if d['rev']:
                b = b + jnp.where(pos < L - step, pltpu.roll(b, n - step, axis=0), 0.0)
            else:
                b = b + jnp.where(pos >= step, pltpu.roll(b, step, axis=0), 0.0)
            step *= 2
        ends = [b[c * L:c * L + 1, :] if d['rev'] else b[(c + 1) * L - 1:(c + 1) * L, :] for c in range(nchunk)]
        b_end = jnp.concatenate([jnp.broadcast_to(e, (L, e.shape[1])) for e in ends], axis=0)
        k = d['k'][...]
        d['ends'] = ends
        d['qe'] = (d['q'][...] * jnp.exp(b)).astype(BF16)
        d['ke'] = (k * jnp.exp(-b)).astype(BF16)
        d['kd'] = (k * jnp.exp(b_end - b)).astype(BF16)
        d['vv'] = d['v'][...]
    for d in dirs:
        d['a'] = [_dot_nt(d['qe'][r], d['ke'][r]) for r in chunks]
    for d in dirs:
        d['a'] = [jnp.where(d['keep'], x, 0.0).astype(BF16) for x in d['a']]
    for d in dirs:
        d['intra'] = [_dot(d['a'][c], d['vv'][chunks[c]]) for c in range(nchunk)]
        d['upd'] = [_dot_tn(d['vv'][r], d['kd'][r]) for r in chunks]
    sts = [d['st'][...] for d in dirs]
    outs = [[None] * nchunk for _ in dirs]
    for s in range(nchunk):
        for i, d in enumerate(dirs):
            c = nchunk - 1 - s if d['rev'] else s
            outs[i][c] = (d['intra'][c] + _dot_nt(d['qe'][chunks[c]], sts[i].astype(BF16))).astype(d['o'].dtype)
            sts[i] = sts[i] * jnp.exp(d['ends'][c]) + d['upd'][c]
    for i, d in enumerate(dirs):
        d['st'][...] = sts[i]
        d['o'][...] = jnp.concatenate(outs[i], axis=0)


def gla_scan(q, k, gf, gb, v):
    B, T, nk = q.shape
    nv = v.shape[-1]
    dk, dv = nk // GLA_HEADS, nv // GLA_HEADS
    tc = _tile(T, 1024)
    nt = T // tc
    qf = pl.BlockSpec((None, tc, dk), lambda b, h, t: (b, t, h))
    vf = pl.BlockSpec((None, tc, dv), lambda b, h, t: (b, t, h))
    qb = pl.BlockSpec((None, tc, dk), lambda b, h, t: (b, nt - 1 - t, h))
    vb = pl.BlockSpec((None, tc, dv), lambda b, h, t: (b, nt - 1 - t, h))
    return pl.pallas_call(
        _gla_kernel,
        grid=(B, GLA_HEADS, nt),
        in_specs=[qf, qf, qf, vf, qb, qb, qb, vb],
        out_specs=[vf, vb],
        out_shape=[jax.ShapeDtypeStruct((B, T, nv), F32)] * 2,
        scratch_shapes=[pltpu.VMEM((dv, dk), F32), pltpu.VMEM((dv, dk), F32)],
        compiler_params=_cparams(("parallel", "parallel", "arbitrary")),
        name="gla_scan",
    )(q, k, gf, v, q, k, gb, v)


def _store_row_tiles(ref, x):
    m = x.shape[0]
    nc = x.shape[1] // LANES
    for c in range(nc):
        ref[pl.ds(c, m, stride=nc), :] = x[:, c * LANES:(c + 1) * LANES]


def _load_row_tiles(ref, nc):
    m = ref.shape[0] // nc
    return jnp.concatenate([ref[pl.ds(c, m, stride=nc), :] for c in range(nc)], axis=1)


def _gla_router_kernel(of_ref, ob_ref, r_ref, on_ref, w_ref, x_ref, mod_ref, ng_ref, whi_ref, wlo_ref,
                       x1_ref, h_ref, idx_ref, gate_ref):
    o = of_ref[...] + ob_ref[...]
    dv = on_ref.shape[-1]
    on = jnp.concatenate([_rms(o[:, hd * dv:(hd + 1) * dv], on_ref[...]) for hd in range(GLA_HEADS)], axis=1)
    y = _dot((on * _silu(r_ref[...].astype(F32))).astype(BF16), w_ref[...])
    x = x_ref[...] + mod_ref[2:3, :] * _rms(y, ng_ref[1:2, :])
    x1_ref[...] = x
    h = _modulate(x, ng_ref[2:3, :], mod_ref[3:4, :], mod_ref[4:5, :])
    _store_row_tiles(h_ref, h)
    h_hi = h.astype(BF16)
    h_lo = (h - h_hi.astype(F32)).astype(BF16)
    logits = _dot_nt(whi_ref[...], h_hi) + _dot_nt(whi_ref[...], h_lo) + _dot_nt(wlo_ref[...], h_hi)
    row = lax.broadcasted_iota(jnp.int32, logits.shape, 0)
    nrow = logits.shape[0]
    logits = jnp.where(row < N_EXPERTS, logits, -jnp.inf)
    m1 = jnp.max(logits, axis=0, keepdims=True)
    i1 = jnp.min(jnp.where(logits == m1, row, nrow), axis=0, keepdims=True)
    rest = jnp.where(row == i1, -jnp.inf, logits)
    m2 = jnp.max(rest, axis=0, keepdims=True)
    i2 = jnp.min(jnp.where(rest == m2, row, nrow), axis=0, keepdims=True)
    e = jnp.exp(m2 - m1)
    idx_ref[...] = jnp.concatenate([i1, i2], axis=0)
    gate_ref[...] = jnp.concatenate([1.0 / (1.0 + e), e / (1.0 + e)], axis=0)


def gla_router(o_f, o_b, r, out_norm, w_out, x, mod, ng, w_router):
    B, T, D = x.shape
    tm = _tile(T, 512)
    nv = o_f.shape[-1]
    wp = jnp.pad(w_router.T, ((0, BF16_SUBLANES - N_EXPERTS), (0, 0)))
    whi = wp.astype(BF16)
    wlo = (wp - whi.astype(F32)).astype(BF16)
    full = lambda w: pl.BlockSpec(w.shape, lambda bb, i: (0,) * w.ndim)
    row = lambda n: pl.BlockSpec((None, tm, n), lambda bb, i: (bb, i, 0))
    top = pl.BlockSpec((None, TOP_K, tm), lambda bb, i: (bb, 0, i))
    return pl.pallas_call(
        _gla_router_kernel,
        grid=(B, T // tm),
        in_specs=[row(nv), row(nv), row(nv), full(out_norm), full(w_out), row(D),
                  pl.BlockSpec((None, 6, D), lambda bb, i: (bb, 0, 0)), full(ng), full(whi), full(wlo)],
        out_specs=[row(D), pl.BlockSpec((None, tm * (D // LANES), LANES), lambda bb, i: (bb, i, 0)), top, top],
        out_shape=[jax.ShapeDtypeStruct((B, T, D), F32), jax.ShapeDtypeStruct((B, T * (D // LANES), LANES), F32),
                   jax.ShapeDtypeStruct((B, TOP_K, T), jnp.int32), jax.ShapeDtypeStruct((B, TOP_K, T), F32)],
        compiler_params=_cparams(("parallel", "parallel")),
        name="gla_router",
    )(o_f, o_b, r, out_norm, w_out, x, mod, ng, whi, wlo)


def _moe_kernel(blk_e_ref, tok_ref, dst_ref, h_hbm, wg_ref, wu_ref, wd_ref, out_hbm,
                xbuf, ybuf, gsem, ssem, *, bm, nh):
    i = pl.program_id(0)
    nb = pl.num_programs(0)
    slot = lax.rem(i, 2)
    other = 1 - slot
    nc = xbuf.shape[1] // bm

    def tile(row):
        return pl.ds(pl.multiple_of(row * nc, nc), nc)

    def gather(blk, r, s):
        return pltpu.make_async_copy(h_hbm.at[tile(tok_ref[blk * bm + r])], xbuf.at[s, tile(r)], gsem.at[s])

    def scatter(blk, r, s):
        return pltpu.make_async_copy(ybuf.at[s, tile(r)], out_hbm.at[tile(dst_ref[(blk + 1) * bm + r])], ssem.at[s])

    def wait_rows(buf, sem, s):
        pltpu.make_async_copy(h_hbm.at[pl.ds(0, bm * nc)], buf.at[s], sem.at[s]).wait()

    @pl.when(i == 0)
    def _():
        def start(r, c):
            gather(0, r, 0).start()
            return c

        lax.fori_loop(0, bm, start, 0, unroll=8)
        ybuf[1] = jnp.zeros(ybuf.shape[1:], F32)

    wait_rows(xbuf, gsem, slot)
    x = _load_row_tiles(xbuf.at[slot], nc).astype(BF16)

    nxt = jnp.minimum(i + 1, nb - 1)
    for r in range(bm):
        gather(nxt, r, other).start()
        scatter(i - 1, r, other).start()

    tf = wg_ref.shape[1] // nh
    y = None
    for c in range(nh):
        cols = slice(c * tf, (c + 1) * tf)
        a = (_silu(_dot(x, wg_ref[:, cols])) * _dot(x, wu_ref[:, cols])).astype(BF16)
        part = _dot(a, wd_ref[cols, :])
        y = part if y is None else y + part

    @pl.when(i >= 1)
    def _():
        wait_rows(ybuf, ssem, slot)

    _store_row_tiles(ybuf.at[slot], y)

    @pl.when(i == nb - 1)
    def _():
        def start(r, c):
            scatter(i, r, slot).start()
            return c

        lax.fori_loop(0, bm, start, 0, unroll=8)
        wait_rows(ybuf, ssem, slot)
        wait_rows(ybuf, ssem, other)
        wait_rows(xbuf, gsem, other)


def moe_experts(h, tok_buf, dst, blk_e, wg, wu, wd, bm, out_rows):
    D = wg.shape[1]
    C = D // LANES
    rows = tok_buf.shape[0]
    E, _, F = wg.shape
    nh = 2 if F % (4 * LANES) == 0 else 1
    wspec = lambda shape: pl.BlockSpec((None,) + shape, lambda i, be, tk, ds: (be[i], 0, 0),
                                       pipeline_mode=pl.Buffered(1))
    return pl.pallas_call(
        functools.partial(_moe_kernel, bm=bm, nh=nh),
        grid_spec=pltpu.PrefetchScalarGridSpec(
            num_scalar_prefetch=3,
            grid=(rows // bm,),
            in_specs=[pl.BlockSpec(memory_space=pl.ANY), wspec((D, F)), wspec((D, F)), wspec((F, D))],
            out_specs=pl.BlockSpec(memory_space=pl.ANY),
            scratch_shapes=[pltpu.VMEM((2, bm * C, LANES), F32), pltpu.VMEM((2, bm * C, LANES), F32),
                            pltpu.SemaphoreType.DMA((2,)), pltpu.SemaphoreType.DMA((2,))],
        ),
        out_shape=jax.ShapeDtypeStruct((out_rows * C, LANES), F32),
        compiler_params=_cparams(("arbitrary",)),
        name="moe_experts",
    )(blk_e, tok_buf, dst, h, wg, wu, wd)


def _moe_combine_kernel(y0_ref, y1_ref, gate_ref, x_ref, mod_ref, ng_ref, o_ref):
    gate = gate_ref[...]
    nc = x_ref.shape[-1] // LANES
    y = _load_row_tiles(y0_ref, nc) * gate[:, 0:1] + _load_row_tiles(y1_ref, nc) * gate[:, 1:2]
    o_ref[...] = x_ref[...] + mod_ref[5:6, :] * _rms(y, ng_ref[3:4, :])


def moe_combine(y, gates, x, mod, ng):
    B, T, D = x.shape
    tm = _tile(T, 512)
    nt = T // tm
    yspec = lambda k: pl.BlockSpec((tm * (D // LANES), LANES), lambda b, i: (k * B * nt + b * nt + i, 0))
    row = lambda n: pl.BlockSpec((None, tm, n), lambda b, i: (b, i, 0))
    return pl.pallas_call(
        _moe_combine_kernel,
        grid=(B, nt),
        in_specs=[yspec(0), yspec(1), row(TOP_K), row(D), pl.BlockSpec((None, 6, D), lambda b, i: (b, 0, 0)),
                  pl.BlockSpec(ng.shape, lambda b, i: (0, 0))],
        out_specs=row(D),
        out_shape=jax.ShapeDtypeStruct((B, T, D), F32),
        compiler_params=_cparams(("parallel", "parallel")),
        name="moe_combine",
    )(y, y, gates, x, mod, ng)


def _routing_tables(top_idx, bm):
    n = top_idx.shape[0]
    flat_e = top_idx.reshape(-1)
    onehot = (flat_e[:, None] == jnp.arange(N_EXPERTS, dtype=jnp.int32)[None, :]).astype(jnp.int32)
    csum = jnp.cumsum(onehot, axis=0)
    rank = jnp.sum(csum * onehot, axis=1) - 1
    counts = csum[-1]
    padded = (counts + bm - 1) // bm * bm
    pad_end = jnp.cumsum(padded)
    pad_start = pad_end - padded
    pos = jnp.sum(pad_start[None, :] * onehot, axis=1) + rank
    rows = n * TOP_K + N_EXPERTS * bm
    flat = jnp.arange(n * TOP_K, dtype=jnp.int32)
    src = jnp.full((rows,), -1, jnp.int32).at[pos].set(flat, unique_indices=True)
    is_pad = src < 0
    tok_buf = jnp.where(is_pad, 0, src // TOP_K)
    spare = n * TOP_K + jnp.cumsum(is_pad.astype(jnp.int32)) - 1
    dst = jnp.where(is_pad, spare, (src % TOP_K) * n + src // TOP_K)
    dst = jnp.concatenate([rows + jnp.arange(bm, dtype=jnp.int32), dst])
    blk_start = jnp.arange(rows // bm, dtype=jnp.int32) * bm
    blk_e = jnp.minimum(jnp.searchsorted(pad_end, blk_start, side='right'), N_EXPERTS - 1).astype(jnp.int32)
    return tok_buf, dst.astype(jnp.int32), blk_e, rows + bm


def gla_out_moe(o_f, o_b, r, out_norm, w_out, x, mod, ng, w_router, wg, wu, wd, bm):
    B, T, D = x.shape
    x, h, top_idx, gates = gla_router(o_f, o_b, r, out_norm, w_out, x, mod, ng, w_router)
    top_idx = jnp.swapaxes(top_idx, 1, 2)
    gates = jnp.swapaxes(gates, 1, 2)
    tok_buf, dst, blk_e, out_rows = _routing_tables(top_idx.reshape(B * T, TOP_K), bm)
    y = moe_experts(h.reshape(-1, LANES), tok_buf, dst, blk_e, wg, wu, wd, bm, out_rows)
    return moe_combine(y, gates, x, mod, ng)


def _prep_weights(p):
    w = {}
    w['l0'] = _prep_l0_weights(p['w_in_ab'][0], p['mla_w_qb'][0], p['mla_w_kvb'][0])
    w['qn'] = p['mla_q_norm'][0][None, :]
    w['kvn'] = p['mla_kv_norm'][0][None, :]
    w_out = p['w_out_ab'][0]
    na = MLA_HEADS * MLA_V
    g = SWA_HEADS // SWA_KV_HEADS
    D = w_out.shape[1]
    wa = jnp.pad(w_out[:na].reshape(MLA_HEADS, MLA_V, D), ((0, 0), (0, LANES - MLA_V), (0, 0)))
    w['wa'] = wa.reshape(MLA_HEADS * LANES, D).astype(BF16)
    w['wb'] = w_out[na:].reshape(SWA_KV_HEADS, g, SWA_HEAD_DIM, D).transpose(1, 0, 2, 3).reshape(-1, D).astype(BF16)
    w['ffn'] = tuple(p[k][0].astype(BF16) for k in ('ffn_w_gate', 'ffn_w_up', 'ffn_w_down'))
    w_in_c = p['w_in_c'][0]
    ncol = w_in_c.shape[1]
    w['win_c'] = jnp.pad(w_in_c, ((0, 0), (0, LANES - 2 * GLA_GATE_RANK))).astype(BF16)
    wgu = p['gla_w_gate_up'][0]
    nk = wgu.shape[-1]
    zero = jnp.zeros((GLA_GATE_RANK, nk), F32)
    wgate = jnp.concatenate([jnp.concatenate([wgu[0], zero], axis=1), jnp.concatenate([zero, wgu[1]], axis=1)], axis=0)
    w['wgate'] = jnp.pad(wgate, ((0, LANES - 2 * GLA_GATE_RANK), (0, 0))).astype(BF16)
    w['bgate'] = p['gla_b_gate_up'][0].reshape(1, 2 * nk)
    w['nk'] = nk
    w['nv'] = (ncol - 2 * nk - 2 * GLA_GATE_RANK) // 2
    w['out_norm'] = p['gla_out_norm'][0][None, :]
    w['w_out_c'] = p['w_out_c'][0].astype(BF16)
    w['router'] = p['moe_router'][0]
    w['moe'] = tuple(p[k][0].astype(BF16) for k in ('moe_w_gate', 'moe_w_up', 'moe_w_down'))
    w['swa_bias'] = swa_bias(p['rel_bias'])
    w['sink'] = p['swa_sink'][0]
    return w


def _trunk(x, c, p, w, moe_bm):
    B, T, D = x.shape
    tables = _rope_tables(T)
    mod = ada_mod(c, p['w_ada'], p['b_ada'], 0)
    ng = p['norm_gain'][0]
    q, k, v, sq, sk, sv = l0_proj(x, mod, ng, w['l0'][0], w['qn'], w['kvn'], *w['l0'][1:], tables)
    a = mla_attention(q, k, v)
    b = swa_attention(sq, sk, sv, w['sink'], w['swa_bias'])
    x = mix_ffn(a, b, w['wa'], w['wb'], x, mod, ng, *w['ffn'])
    mod = ada_mod(c, p['w_ada'], p['b_ada'], 1)
    ng = p['norm_gain'][1]
    q, k, v, r, gf, gb = l1_proj(x, mod, ng, w['win_c'], w['wgate'], w['bgate'], w['nk'], w['nv'])
    o_f, o_b = gla_scan(q, k, gf, gb, v)
    return gla_out_moe(o_f, o_b, r, w['out_norm'], w['w_out_c'], x, mod, ng, w['router'], *w['moe'], moe_bm)


def kernel(x_prompt, x_sample, c_prompt, c_sample, norm_gain, w_ada, b_ada, rel_bias, w_in_ab, mla_q_norm, mla_kv_norm, mla_w_qb, mla_w_kvb, swa_sink, w_out_ab, ffn_w_gate, ffn_w_up, ffn_w_down, w_in_c, gla_w_gate_up, gla_b_gate_up, gla_out_norm, w_out_c, moe_router, moe_w_gate, moe_w_up, moe_w_down):
    p = dict(norm_gain=norm_gain, w_ada=w_ada, b_ada=b_ada, rel_bias=rel_bias, w_in_ab=w_in_ab,
             mla_q_norm=mla_q_norm, mla_kv_norm=mla_kv_norm, mla_w_qb=mla_w_qb, mla_w_kvb=mla_w_kvb,
             swa_sink=swa_sink, w_out_ab=w_out_ab, ffn_w_gate=ffn_w_gate, ffn_w_up=ffn_w_up,
             ffn_w_down=ffn_w_down, w_in_c=w_in_c, gla_w_gate_up=gla_w_gate_up, gla_b_gate_up=gla_b_gate_up,
             gla_out_norm=gla_out_norm, w_out_c=w_out_c, moe_router=moe_router, moe_w_gate=moe_w_gate,
             moe_w_up=moe_w_up, moe_w_down=moe_w_down)
    w = _prep_weights(p)
    bm = 512
    return (_trunk(x_prompt, c_prompt, p, w, bm), _trunk(x_sample, c_sample, p, w, bm))
```
